```python
import jax
import jax.numpy as jnp
from jax import lax
import numpy as np

D_MODEL = 1024
BATCH = 8
SEQ = 2048
DEPTH = 4

N_MEM = 256
EPS = 1e-6
Q_BLOCK = 128
FFN_DIM = 2816

A_HEADS = 4
A_Q_LORA = 384
A_KV_LORA = 256
A_NOPE = 128
A_ROPE = 64
A_V = 128
ROPE_BASE = 10000.0

B_HEADS = 4
B_HEAD_DIM = 128
B_WIDTH = B_HEADS * B_HEAD_DIM
B_CONV = 4
B_CHUNK = 128

C_HEADS = 8
C_Q_LORA = 384
C_KV_LORA = 256
C_NOPE = 128
C_V = 128
IDX_HEADS = 8
IDX_DIM = 64
TOPK_MAX = 256

X_HEADS = 4
X_HEAD_DIM = 128

AB_IN = A_Q_LORA + A_KV_LORA + A_ROPE + 3 * B_WIDTH
AB_SPLITS = (A_Q_LORA, A_Q_LORA + A_KV_LORA, A_Q_LORA + A_KV_LORA + A_ROPE,
             A_Q_LORA + A_KV_LORA + A_ROPE + B_WIDTH, A_Q_LORA + A_KV_LORA + A_ROPE + 2 * B_WIDTH)
AB_MIX = A_HEADS * A_V + B_WIDTH
C_IN = C_Q_LORA + C_KV_LORA + IDX_HEADS * IDX_DIM + IDX_DIM + IDX_HEADS
C_SPLITS = (C_Q_LORA, C_Q_LORA + C_KV_LORA, C_Q_LORA + C_KV_LORA + IDX_HEADS * IDX_DIM,
            C_Q_LORA + C_KV_LORA + IDX_HEADS * IDX_DIM + IDX_DIM)
N_EVEN = (DEPTH + 1) // 2
N_ODD = DEPTH // 2

kernel_name = 'hybrid_mla_mlstm_dsa_macaron'


def rmsnorm(x, g):
    xf = x.astype(jnp.float32)
    y = xf * lax.rsqrt(jnp.mean(xf * xf, -1, keepdims=True) + EPS)
    return (y * g.astype(jnp.float32)).astype(x.dtype)


def swiglu(x, wg, wu, wd):
    return (jax.nn.silu(x @ wg) * (x @ wu)) @ wd


def rope(x, positions):
    half = x.shape[-1] // 2
    inv = ROPE_BASE ** (-jnp.arange(half, dtype=jnp.float32) / half)
    ang = positions.astype(jnp.float32)[..., None] * inv
    ang = ang.reshape(ang.shape[:2] + (1,) * (x.ndim - 3) + (half,))
    cos, sin = jnp.cos(ang), jnp.sin(ang)
    xf = x.astype(jnp.float32)
    x1, x2 = xf[..., :half], xf[..., half:]
    return jnp.concatenate([x1 * cos - x2 * sin, x1 * sin + x2 * cos], -1).astype(x.dtype)


def causal_block_attention(q, k, v, scale):
    seq = q.shape[2]
    tri = jnp.tril(jnp.ones((Q_BLOCK, Q_BLOCK), bool))
    outs = []
    for i0 in range(0, seq, Q_BLOCK):
        end = i0 + Q_BLOCK
        s = jnp.einsum('bhqd,bhkd->bhqk', q[:, :, i0:end], k[:, :, :end]).astype(jnp.float32) * scale
        mask = jnp.concatenate([jnp.ones((Q_BLOCK, i0), bool), tri], axis=1)
        p = jax.nn.softmax(jnp.where(mask, s, -jnp.inf), axis=-1).astype(v.dtype)
        outs.append(jnp.einsum('bhqk,bhkd->bhqd', p, v[:, :, :end]))
    return jnp.concatenate(outs, axis=2)


def mla_mixer(cq, ckv, kr, positions, cq_g, ckv_g, w_uq, w_ukv):
    b, s, _ = cq.shape
    q = (rmsnorm(cq, cq_g) @ w_uq).reshape(b, s, A_HEADS, A_NOPE + A_ROPE)
    q_nope, q_rope = q[..., :A_NOPE], rope(q[..., A_NOPE:], positions)
    kv = (rmsnorm(ckv, ckv_g) @ w_ukv).reshape(b, s, A_HEADS, A_NOPE + A_V)
    k_nope, v = kv[..., :A_NOPE], kv[..., A_NOPE:]
    k_rope = jnp.broadcast_to(rope(kr, positions)[:, :, None, :], (b, s, A_HEADS, A_ROPE))
    qf = jnp.concatenate([q_nope, q_rope], -1).transpose(0, 2, 1, 3)
    kf = jnp.concatenate([k_nope, k_rope], -1).transpose(0, 2, 1, 3)
    o = causal_block_attention(qf, kf, v.transpose(0, 2, 1, 3), (A_NOPE + A_ROPE) ** -0.5)
    return o.transpose(0, 2, 1, 3).reshape(b, s, A_HEADS * A_V)


def causal_dwconv(x, w, bias):
    y = lax.conv_general_dilated(x, w[:, None, :], window_strides=(1,), padding=[(B_CONV - 1, 0)],
                                 dimension_numbers=('NWC', 'WIO', 'NWC'), feature_group_count=x.shape[-1])
    return y + bias


def mlstm_chunkwise(q, k, v, li, lf):
    b, h, s, d = q.shape
    nc = s // B_CHUNK

    def to_chunks(t):
        return jnp.moveaxis(t.reshape((b, h, nc, B_CHUNK) + t.shape[3:]), 2, 0)

    causal = jnp.tril(jnp.ones((B_CHUNK, B_CHUNK), bool))

    def step(carry, inp):
        c_st, n_st, m_st = carry
        qj, kj, vj, lij, lfj = inp
        bcum = jnp.cumsum(lfj, -1)
        dmat = bcum[..., :, None] - bcum[..., None, :] + lij[..., None, :]
        dmat = jnp.where(causal, dmat, -jnp.inf)
        m_inter = bcum + m_st[..., None]
        m_j = jnp.maximum(m_inter, jnp.max(dmat, -1))
        sc = jnp.einsum('bhld,bhsd->bhls', qj, kj) * jnp.exp(dmat - m_j[..., None])
        inter = jnp.exp(m_inter - m_j)
        num = inter[..., None] * jnp.einsum('bhld,bhdv->bhlv', qj, c_st) + jnp.einsum('bhls,bhsv->bhlv', sc, vj)
        den = inter * jnp.einsum('bhld,bhd->bhl', qj, n_st) + jnp.sum(sc, -1)
        hj = num / jnp.maximum(jnp.abs(den), jnp.exp(-m_j))[..., None]
        b_tot = bcum[..., -1]
        g = b_tot[..., None] - bcum + lij
        m_new = jnp.maximum(b_tot + m_st, jnp.max(g, -1))
        decay = jnp.exp(b_tot + m_st - m_new)
        wg = jnp.exp(g - m_new[..., None])
        c_new = decay[..., None, None] * c_st + jnp.einsum('bhs,bhsd,bhsv->bhdv', wg, kj, vj)
        n_new = decay[..., None] * n_st + jnp.einsum('bhs,bhsd->bhd', wg, kj)
        return (c_new, n_new, m_new), hj

    init = (jnp.zeros((b, h, d, d), jnp.float32), jnp.zeros((b, h, d), jnp.float32),
            jnp.zeros((b, h), jnp.float32))
    _, hs = lax.scan(step, init, (to_chunks(q), to_chunks(k), to_chunks(v), to_chunks(li), to_chunks(lf)))
    return jnp.moveaxis(hs, 0, 2).reshape(b, h, s, d)


def mlstm_mixer(xm, vm, og, conv_w, conv_b, wq, wk, w_gates, b_gates, gn_g, skip):
    b, s, _ = xm.shape
    xc = jax.nn.silu(causal_dwconv(xm, conv_w, conv_b))
    xch = xc.reshape(b, s, B_HEADS, B_HEAD_DIM)
    q = jnp.einsum('bshd,hde->bshe', xch, wq)
    k = jnp.einsum('bshd,hde->bshe', xch, wk)
    gates = (jnp.concatenate([q.reshape(b, s, B_WIDTH), k.reshape(b, s, B_WIDTH), vm], -1) @ w_gates
             + b_gates).astype(jnp.float32)
    li = gates[..., :B_HEADS]
    lf = jax.nn.log_sigmoid(gates[..., B_HEADS:])
    to_bhs = lambda t: jnp.moveaxis(t.astype(jnp.float32), 2, 1)
    hh = mlstm_chunkwise(to_bhs(q), to_bhs(k) * B_HEAD_DIM ** -0.5,
                         to_bhs(vm.reshape(b, s, B_HEADS, B_HEAD_DIM)), to_bhs(li), to_bhs(lf))
    mu = jnp.mean(hh, -1, keepdims=True)
    var = jnp.mean(jnp.square(hh - mu), -1, keepdims=True)
    hn = ((hh - mu) * lax.rsqrt(var + EPS)).transpose(0, 2, 1, 3).reshape(b, s, B_WIDTH)
    out = ((hn * gn_g.astype(jnp.float32) + skip.astype(jnp.float32) * xc.astype(jnp.float32))
           * jax.nn.sigmoid(og.astype(jnp.float32)))
    return out.astype(xm.dtype)


def dsa_mixer(h, w_in, cq_g, ckv_g, w_uq, w_uk, w_uv):
    b, s, _ = h.shape
    cq, ckv, qi, ki, wi = jnp.split(h @ w_in, C_SPLITS, axis=-1)
    q = (rmsnorm(cq, cq_g) @ w_uq).reshape(b, s, C_HEADS, C_NOPE)
    ckv = rmsnorm(ckv, ckv_g)
    q_lat = jnp.einsum('bshd,hcd->bshc', q, w_uk)
    qi = qi.reshape(b, s, IDX_HEADS, IDX_DIM)
    topk = min(TOPK_MAX, s // 4)
    scale = C_NOPE ** -0.5
    outs = []
    for i0 in range(0, s, Q_BLOCK):
        end = i0 + Q_BLOCK
        kk = min(topk, end)
        t_pos = jnp.arange(i0, end)
        admissible = jnp.arange(end)[None, :] <= t_pos[:, None]
        rel = jax.nn.relu(jnp.einsum('bthd,bsd->bths', qi[:, i0:end], ki[:, :end]))
        score = jnp.einsum('bths,bth->bts', rel, wi[:, i0:end]).astype(jnp.float32)
        _, idx = lax.top_k(jnp.where(admissible, score, -jnp.inf), kk)
        valid = idx <= t_pos[None, :, None]
        c_sel = jax.vmap(lambda c, i: c[i])(ckv[:, :end], idx)
        logits = jnp.einsum('bthc,btkc->bthk', q_lat[:, i0:end], c_sel).astype(jnp.float32) * scale
        p = jax.nn.softmax(jnp.where(valid[:, :, None, :], logits, -jnp.inf), -1).astype(c_sel.dtype)
        outs.append(jnp.einsum('bthk,btkc->bthc', p, c_sel))
    o_lat = jnp.concatenate(outs, 1)
    return jnp.einsum('bshc,hcv->bshv', o_lat, w_uv).reshape(b, s, C_HEADS * C_V)


def memory_cross_attention(h, mem_n, wq, wk, wv, wo):
    b, s, _ = h.shape
    q = (h @ wq).reshape(b, s, X_HEADS, X_HEAD_DIM)
    k = (mem_n @ wk).reshape(b, -1, X_HEADS, X_HEAD_DIM)
    v = (mem_n @ wv).reshape(b, -1, X_HEADS, X_HEAD_DIM)
    logits = jnp.einsum('bshd,bmhd->bhsm', q, k).astype(jnp.float32) * X_HEAD_DIM ** -0.5
    p = jax.nn.softmax(logits, -1).astype(v.dtype)
    return jnp.einsum('bhsm,bmhd->bshd', p, v).reshape(b, s, X_HEADS * X_HEAD_DIM) @ wo


def setup_inputs(seed: int = 0) -> dict:
    key = jax.random.key(seed)
    ks = iter(jax.random.split(key, 64))
    f32 = jnp.float32
    nrm = lambda shape, fan_in: jax.random.normal(next(ks), shape, f32) * fan_in ** -0.5
    gain = lambda shape: 1.0 + 0.05 * jax.random.normal(next(ks), shape, f32)
    small = lambda shape: 0.02 * jax.random.normal(next(ks), shape, f32)
    x = jax.random.normal(next(ks), (BATCH, SEQ, D_MODEL), f32)
    mem = jax.random.normal(next(ks), (BATCH, N_MEM, D_MODEL), f32)
    positions = (jax.random.randint(next(ks), (BATCH, 1), 0, 1024, jnp.int32)
                 + jnp.arange(SEQ, dtype=jnp.int32)[None, :])
    b_gates = jnp.concatenate([
        0.1 * jax.random.normal(next(ks), (N_EVEN, B_HEADS), f32),
        jnp.linspace(3.0, 6.0, B_HEADS, dtype=f32)[None, :]
        + 0.1 * jax.random.normal(next(ks), (N_EVEN, B_HEADS), f32)], axis=-1)
    return {
        'x': x,
        'mem': mem,
        'positions': positions,
        'norm_g': gain((DEPTH, 8, D_MODEL)),
        'mem_norm_g': gain((DEPTH, D_MODEL)),
        'ffn_w_gate': nrm((DEPTH, 2, D_MODEL, FFN_DIM), D_MODEL),
        'ffn_w_up': nrm((DEPTH, 2, D_MODEL, FFN_DIM), D_MODEL),
        'ffn_w_down': nrm((DEPTH, 2, FFN_DIM, D_MODEL), FFN_DIM),
        'xa_wq': nrm((DEPTH, D_MODEL, X_HEADS * X_HEAD_DIM), D_MODEL),
        'xa_wk': nrm((DEPTH, D_MODEL, X_HEADS * X_HEAD_DIM), D_MODEL),
        'xa_wv': nrm((DEPTH, D_MODEL, X_HEADS * X_HEAD_DIM), D_MODEL),
        'xa_wo': nrm((DEPTH, X_HEADS * X_HEAD_DIM, D_MODEL), X_HEADS * X_HEAD_DIM),
        'ab_w_in': nrm((N_EVEN, D_MODEL, AB_IN), D_MODEL),
        'ab_w_out': nrm((N_EVEN, AB_MIX, D_MODEL), AB_MIX),
        'mla_cq_g': gain((N_EVEN, A_Q_LORA)),
        'mla_ckv_g': gain((N_EVEN, A_KV_LORA)),
        'mla_w_uq': nrm((N_EVEN, A_Q_LORA, A_HEADS * (A_NOPE + A_ROPE)), A_Q_LORA),
        'mla_w_ukv': nrm((N_EVEN, A_KV_LORA, A_HEADS * (A_NOPE + A_V)), A_KV_LORA),
        'ml_conv_w': nrm((N_EVEN, B_CONV, B_WIDTH), B_CONV),
        'ml_conv_b': small((N_EVEN, B_WIDTH)),
        'ml_wq': nrm((N_EVEN, B_HEADS, B_HEAD_DIM, B_HEAD_DIM), B_HEAD_DIM),
        'ml_wk': nrm((N_EVEN, B_HEADS, B_HEAD_DIM, B_HEAD_DIM), B_HEAD_DIM),
        'ml_w_gates': nrm((N_EVEN, 3 * B_WIDTH, 2 * B_HEADS), 3 * B_WIDTH),
        'ml_b_gates': b_gates,
        'ml_gn_g': gain((N_EVEN, B_WIDTH)),
        'ml_skip': gain((N_EVEN, B_WIDTH)),
        'c_w_in': nrm((N_ODD, D_MODEL, C_IN), D_MODEL),
        'c_w_out': nrm((N_ODD, C_HEADS * C_V, D_MODEL), C_HEADS * C_V),
        'c_cq_g': gain((N_ODD, C_Q_LORA)),
        'c_ckv_g': gain((N_ODD, C_KV_LORA)),
        'c_w_uq': nrm((N_ODD, C_Q_LORA, C_HEADS * C_NOPE), C_Q_LORA),
        'c_w_uk': nrm((N_ODD, C_HEADS, C_KV_LORA, C_NOPE), C_KV_LORA),
        'c_w_uv': nrm((N_ODD, C_HEADS, C_KV_LORA, C_V), C_KV_LORA),
    }


def reference(x, mem, positions, norm_g, mem_norm_g, ffn_w_gate, ffn_w_up, ffn_w_down,
              xa_wq, xa_wk, xa_wv, xa_wo, ab_w_in, ab_w_out, mla_cq_g, mla_ckv_g, mla_w_uq, mla_w_ukv,
              ml_conv_w, ml_conv_b, ml_wq, ml_wk, ml_w_gates, ml_b_gates, ml_gn_g, ml_skip,
              c_w_in, c_w_out, c_cq_g, c_ckv_g, c_w_uq, c_w_uk, c_w_uv):
    for l in range(DEPTH):
        g = norm_g[l]
        h = rmsnorm(x, g[0])
        x = x + 0.5 * rmsnorm(swiglu(h, ffn_w_gate[l, 0], ffn_w_up[l, 0], ffn_w_down[l, 0]), g[1])
        h = rmsnorm(x, g[2])
        if l % 2 == 0:
            e = l // 2
            cq, ckv, kr, xm, vm, og = jnp.split(h @ ab_w_in[e], AB_SPLITS, axis=-1)
            ya = mla_mixer(cq, ckv, kr, positions, mla_cq_g[e], mla_ckv_g[e], mla_w_uq[e], mla_w_ukv[e])
            yb = mlstm_mixer(xm, vm, og, ml_conv_w[e], ml_conv_b[e], ml_wq[e], ml_wk[e],
                             ml_w_gates[e], ml_b_gates[e], ml_gn_g[e], ml_skip[e])
            y = jnp.concatenate([ya, yb], axis=-1) @ ab_w_out[e]
        else:
            o = l // 2
            y = dsa_mixer(h, c_w_in[o], c_cq_g[o], c_ckv_g[o], c_w_uq[o], c_w_uk[o], c_w_uv[o]) @ c_w_out[o]
        x = x + rmsnorm(y, g[3])
        h = rmsnorm(x, g[4])
        y = memory_cross_attention(h, rmsnorm(mem, mem_norm_g[l]), xa_wq[l], xa_wk[l], xa_wv[l], xa_wo[l])
        x = x + rmsnorm(y, g[5])
        h = rmsnorm(x, g[6])
        x = x + 0.5 * rmsnorm(swiglu(h, ffn_w_gate[l, 1], ffn_w_up[l, 1], ffn_w_down[l, 1]), g[7])
    return x
```

```python
import functools
import math

import jax
import jax.numpy as jnp
from jax import lax
from jax.experimental import pallas as pl
from jax.experimental.pallas import tpu as pltpu

F32 = jnp.float32
BF16 = jnp.bfloat16

EPS = 1e-6
ROPE_BASE = 10000.0
LANES = 128
VMEM_LIMIT = 48 * 1024 * 1024

A_HEADS, A_Q_LORA, A_KV_LORA, A_NOPE, A_ROPE, A_V = 4, 384, 256, 128, 64, 128
B_HEADS, B_HEAD_DIM, B_CONV, B_CHUNK = 4, 128, 4, 128
B_WIDTH = B_HEADS * B_HEAD_DIM
C_HEADS, C_Q_LORA, C_KV_LORA, C_NOPE, C_V = 8, 384, 256, 128, 128
IDX_HEADS, IDX_DIM, TOPK_MAX = 8, 64, 256
X_HEADS, X_HEAD_DIM = 4, 128
Q_BLOCK = 128

INT_MIN = -(2 ** 31)
KEY_LOWEST_FINITE = INT_MIN + 0x00800000


def _params(*sem):
    return pltpu.CompilerParams(dimension_semantics=sem, vmem_limit_bytes=VMEM_LIMIT)


def _rms(x, g):
    return x * lax.rsqrt(jnp.mean(x * x, axis=-1, keepdims=True) + EPS) * g


def _dot(a, b):
    return jnp.dot(a, b, preferred_element_type=F32)


def _dot_nt(a, b):
    return lax.dot_general(a, b, (((1,), (1,)), ((), ())), preferred_element_type=F32)


def _dot_tn(a, b):
    return lax.dot_general(a, b, (((0,), (0,)), ((), ())), preferred_element_type=F32)


def _full(shape):
    return pl.BlockSpec(shape, lambda *_: (0,) * len(shape))


def _ffn_body(x_ref, gpre_ref, gpost_ref, wg_ref, wu_ref, wd_ref, o_ref, h_scr, acc_scr):
    j = pl.program_id(1)

    @pl.when(j == 0)
    def _():
        h_scr[...] = _rms(x_ref[...], gpre_ref[...]).astype(BF16)
        acc_scr[...] = jnp.zeros_like(acc_scr)

    h = h_scr[...]
    g = _dot(h, wg_ref[...])
    u = _dot(h, wu_ref[...])
    a = (g * jax.nn.sigmoid(g) * u).astype(BF16)
    acc_scr[...] += _dot(a, wd_ref[...])

    @pl.when(j == pl.num_programs(1) - 1)
    def _():
        o_ref[...] = x_ref[...] + 0.5 * _rms(acc_scr[...], gpost_ref[...])


def _ffn(x, gpre, gpost, wg, wu, wd, layer, slot, *, tm, tf):
    t, d = x.shape
    f = wg.shape[-1]
    return pl.pallas_call(
        _ffn_body,
        grid=(t // tm, f // tf),
        in_specs=[
            pl.BlockSpec((tm, d), lambda i, j: (i, 0)),
            _full((1, d)), _full((1, d)),
            pl.BlockSpec((None, None, d, tf), lambda i, j: (layer, slot, 0, j)),
            pl.BlockSpec((None, None, d, tf), lambda i, j: (layer, slot, 0, j)),
            pl.BlockSpec((None, None, tf, d), lambda i, j: (layer, slot, j, 0)),
        ],
        out_specs=pl.BlockSpec((tm, d), lambda i, j: (i, 0)),
        out_shape=jax.ShapeDtypeStruct((t, d), F32),
        scratch_shapes=[pltpu.VMEM((tm, d), BF16), pltpu.VMEM((tm, d), F32)],
        compiler_params=_params("parallel", "arbitrary"),
        name="ffn",
    )(x, gpre, gpost, wg, wu, wd)


def _proj_body(n_in, x_ref, g_ref, *refs):
    a_refs, w_refs, o_ref = refs[:n_in], refs[n_in:2 * n_in], refs[2 * n_in]
    y = _dot(a_refs[0][...], w_refs[0][...])
    for a_ref, w_ref in zip(a_refs[1:], w_refs[1:]):
        y += _dot(a_ref[...], w_ref[...])
    o_ref[...] = x_ref[...] + _rms(y, g_ref[...])


def _proj(x, g, acts, weights, *, tm):
    t, d = x.shape
    n_in = len(acts)
    return pl.pallas_call(
        functools.partial(_proj_body, n_in),
        grid=(t // tm,),
        in_specs=[pl.BlockSpec((tm, d), lambda i: (i, 0)), _full((1, d))]
        + [pl.BlockSpec((tm, a.shape[1]), lambda i: (i, 0)) for a in acts]
        + [_full(w.shape) for w in weights],
        out_specs=pl.BlockSpec((tm, d), lambda i: (i, 0)),
        out_shape=jax.ShapeDtypeStruct((t, d), F32),
        compiler_params=_params("parallel"),
        name="mixer_out_proj",
    )(x, g, *acts, *weights)


def _rope_tables(pos):
    half = A_ROPE // 2
    lane = lax.broadcasted_iota(jnp.int32, (1, LANES), 1)
    j = (lane % half).astype(F32)
    inv = jnp.exp(j * (-math.log(ROPE_BASE) / half))
    ang = pos * inv
    sin = jnp.where(lane < A_ROPE, jnp.sin(ang), 0.0)
    return jnp.cos(ang), sin


def _rope_apply(r, cos, sin):
    half = A_ROPE // 2
    return r * cos + (pltpu.roll(r, half, axis=1) - pltpu.roll(r, LANES - half, axis=1)) * sin


AB_Z = A_Q_LORA + A_KV_LORA + LANES + 3 * B_WIDTH
A_HEAD_PAD = 2 * LANES


def _ab_prep_body(x_ref, pos_ref, g_ref, w_in_ref, cq_g_ref, ckv_g_ref, w_uq_ref, w_ukv_ref,
                  q_ref, k_ref, v_ref, xm_ref, vm_ref, og_ref):
    h = _rms(x_ref[...], g_ref[...]).astype(BF16)
    z = _dot(h, w_in_ref[...])
    o = 0
    cq = z[:, o:o + A_Q_LORA]; o += A_Q_LORA
    ckv = z[:, o:o + A_KV_LORA]; o += A_KV_LORA
    kr = z[:, o:o + LANES]; o += LANES
    xm_ref[...] = z[:, o:o + B_WIDTH]; o += B_WIDTH
    vm_ref[...] = z[:, o:o + B_WIDTH]; o += B_WIDTH
    og_ref[...] = z[:, o:o + B_WIDTH]

    cos, sin = _rope_tables(pos_ref[...])
    scale = (A_NOPE + A_ROPE) ** -0.5
    q = _dot(_rms(cq, cq_g_ref[...]).astype(BF16), w_uq_ref[...]) * scale
    kv = _dot(_rms(ckv, ckv_g_ref[...]).astype(BF16), w_ukv_ref[...])
    k_rope = _rope_apply(kr, cos, sin).astype(BF16)
    for hd in range(A_HEADS):
        base = hd * A_HEAD_PAD
        q_ref[:, base:base + A_NOPE] = q[:, base:base + A_NOPE].astype(BF16)
        q_ref[:, base + A_NOPE:base + A_HEAD_PAD] = _rope_apply(
            q[:, base + A_NOPE:base + A_HEAD_PAD], cos, sin).astype(BF16)
        k_ref[:, base:base + A_NOPE] = kv[:, hd * A_NOPE:(hd + 1) * A_NOPE].astype(BF16)
        k_ref[:, base + A_NOPE:base + A_HEAD_PAD] = k_rope
    v_ref[...] = kv[:, A_HEADS * A_NOPE:].astype(BF16)


def _ab_prep(x, pos, g, w_in, cq_g, ckv_g, w_uq, w_ukv, *, tm):
    t, d = x.shape
    row = lambda w: pl.BlockSpec((tm, w), lambda i: (i, 0))
    return pl.pallas_call(
        _ab_prep_body,
        grid=(t // tm,),
        in_specs=[row(d), row(1), _full((1, d)), _full(w_in.shape), _full(cq_g.shape), _full(ckv_g.shape),
                  _full(w_uq.shape), _full(w_ukv.shape)],
        out_specs=[row(A_HEADS * A_HEAD_PAD), row(A_HEADS * A_HEAD_PAD), row(A_HEADS * A_V),
                   row(B_WIDTH), row(B_WIDTH), row(B_WIDTH)],
        out_shape=[jax.ShapeDtypeStruct((t, A_HEADS * A_HEAD_PAD), BF16),
                   jax.ShapeDtypeStruct((t, A_HEADS * A_HEAD_PAD), BF16),
                   jax.ShapeDtypeStruct((t, A_HEADS * A_V), BF16),
                   jax.ShapeDtypeStruct((t, B_WIDTH), F32),
                   jax.ShapeDtypeStruct((t, B_WIDTH), F32),
                   jax.ShapeDtypeStruct((t, B_WIDTH), F32)],
        compiler_params=_params("parallel"),
        name="ab_prep",
    )(x, pos, g, w_in, cq_g, ckv_g, w_uq, w_ukv)


def _mla_attn_body(tq, q_ref, k_ref, v_ref, o_ref):
    i = pl.program_id(2)
    q = q_ref[...]

    def chunk(j, carry, masked):
        m, l, acc = carry
        start = pl.multiple_of(j * tq, tq)
        s = _dot_nt(q, k_ref[pl.ds(start, tq), :])
        if masked:
            row = lax.broadcasted_iota(jnp.int32, (tq, tq), 0)
            col = lax.broadcasted_iota(jnp.int32, (tq, tq), 1)
            s = jnp.where(col <= row, s, -jnp.inf)
        m_new = jnp.maximum(m, jnp.max(s, axis=-1, keepdims=True))
        alpha = jnp.exp(m - m_new)
        p = jnp.exp(s - m_new)
        l = alpha * l + jnp.sum(p, axis=-1, keepdims=True)
        acc = alpha * acc + _dot(p.astype(BF16), v_ref[pl.ds(start, tq), :])
        return m_new, l, acc

    init = (jnp.full((tq, 1), -jnp.inf, F32), jnp.zeros((tq, 1), F32), jnp.zeros((tq, A_V), F32))
    carry = lax.fori_loop(0, i, lambda j, c: chunk(j, c, False), init)
    _, l, acc = chunk(i, carry, True)
    o_ref[...] = (acc / l).astype(BF16)


def _mla_attn(q, k, v, batch, *, tq):
    t = q.shape[0]
    s = t // batch
    nq = s // tq
    return pl.pallas_call(
        functools.partial(_mla_attn_body, tq),
        grid=(batch, A_HEADS, nq),
        in_specs=[pl.BlockSpec((tq, A_HEAD_PAD), lambda b, h, i: (b * nq + i, h)),
                  pl.BlockSpec((s, A_HEAD_PAD), lambda b, h, i: (b, h)),
                  pl.BlockSpec((s, A_V), lambda b, h, i: (b, h))],
        out_specs=pl.BlockSpec((tq, A_V), lambda b, h, i: (b * nq + i, h)),
        out_shape=jax.ShapeDtypeStruct((t, A_HEADS * A_V), BF16),
        compiler_params=_params("parallel", "parallel", "arbitrary"),
        name="mla_attn",
    )(q, k, v)


def _log_sigmoid(x):
    return jnp.minimum(x, 0.0) - jnp.log(1.0 + jnp.exp(-jnp.abs(x)))


def _mlstm_body(xm_ref, vm_ref, og_ref, conv_w_ref, conv_b_ref, wq_ref, wk_ref, wgt_ref, bgt_ref,
                gn_ref, skip_ref, y_ref, xext_scr, gin_scr, state_scr, m_scr):
    c = pl.program_id(1)
    L = B_CHUNK
    tail = 8

    @pl.when(c == 0)
    def _():
        xext_scr[0:tail, :] = jnp.zeros((tail, B_WIDTH), F32)
        state_scr[...] = jnp.zeros_like(state_scr)
        m_scr[...] = jnp.zeros_like(m_scr)

    xm = xm_ref[...]
    xext_scr[tail:tail + L, :] = xm
    conv = conv_b_ref[...] + conv_w_ref[B_CONV - 1:B_CONV, :] * xm
    for tap in range(B_CONV - 1):
        shift = B_CONV - 1 - tap
        conv += conv_w_ref[tap:tap + 1, :] * xext_scr[tail - shift:tail - shift + L, :]
    xext_scr[0:tail, :] = xm[L - tail:, :]
    xc = conv * jax.nn.sigmoid(conv)
    xcb = xc.astype(BF16)

    vm = vm_ref[...]
    qs, ks = [], []
    for h in range(B_HEADS):
        sl = slice(h * B_HEAD_DIM, (h + 1) * B_HEAD_DIM)
        qh = _dot(xcb[:, sl], wq_ref[h])
        kh = _dot(xcb[:, sl], wk_ref[h])
        qs.append(qh)
        ks.append(kh)
        gin_scr[:, h * B_HEAD_DIM:(h + 1) * B_HEAD_DIM] = qh.astype(BF16)
        gin_scr[:, B_WIDTH + h * B_HEAD_DIM:B_WIDTH + (h + 1) * B_HEAD_DIM] = kh.astype(BF16)
    gin_scr[:, 2 * B_WIDTH:] = vm.astype(BF16)
    gates = _dot(gin_scr[...], wgt_ref[...]) + bgt_ref[...]

    lane = lax.broadcasted_iota(jnp.int32, (L, LANES), 1)
    row = lax.broadcasted_iota(jnp.int32, (L, LANES), 0)
    is_f = (lane >= B_HEADS) & (lane < 2 * B_HEADS)
    glog = jnp.where(is_f, _log_sigmoid(gates), gates)
    bcum = jnp.where(is_f, glog, 0.0)
    step = 1
    while step < L:
        bcum = bcum + jnp.where(row >= step, pltpu.roll(bcum, step, axis=0), 0.0)
        step *= 2
    glog_t = glog.T
    bcum_t = bcum.T

    causal = lax.broadcasted_iota(jnp.int32, (L, L), 1) <= lax.broadcasted_iota(jnp.int32, (L, L), 0)
    ones_col = jnp.where(lax.broadcasted_iota(jnp.int32, (L, LANES), 1) == 0, 1.0, 0.0).astype(BF16)
    kscale = B_HEAD_DIM ** -0.5
    for h in range(B_HEADS):
        sl = slice(h * B_HEAD_DIM, (h + 1) * B_HEAD_DIM)
        qb = qs[h].astype(BF16)
        k_s = ks[h] * kscale
        v_ext = jnp.concatenate([vm[:, sl].astype(BF16), ones_col], axis=1)
        li_c = glog[:, h:h + 1]
        b_c = bcum[:, B_HEADS + h:B_HEADS + h + 1]
        li_r = glog_t[h:h + 1, :]
        b_r = bcum_t[B_HEADS + h:B_HEADS + h + 1, :]
        b_tot = b_r[:, L - 1:L]
        m_st = m_scr[h:h + 1, 0:1]
        state = state_scr[h]

        dmat = jnp.where(causal, b_c - b_r + li_r, -jnp.inf)
        m_inter = b_c + m_st
        m_j = jnp.maximum(m_inter, jnp.max(dmat, axis=-1, keepdims=True))
        sc = _dot_nt(qb, k_s.astype(BF16)) * jnp.exp(dmat - m_j)
        inter = jnp.exp(m_inter - m_j)
        ne = inter * _dot(qb, state.astype(BF16)) + _dot(sc.astype(BF16), v_ext)
        den = ne[:, B_HEAD_DIM:B_HEAD_DIM + 1]
        hj = ne[:, :B_HEAD_DIM] / jnp.maximum(jnp.abs(den), jnp.exp(-m_j))

        g_c = b_tot - b_c + li_c
        g_r = b_tot - b_r + li_r
        m_new = jnp.maximum(b_tot + m_st, jnp.max(g_r, axis=-1, keepdims=True))
        decay = jnp.exp(b_tot + m_st - m_new)
        kw = (k_s * jnp.exp(g_c - m_new)).astype(BF16)
        state_scr[h] = decay * state + _dot_tn(kw, v_ext)
        m_scr[h:h + 1, :] = jnp.broadcast_to(m_new, (1, LANES))

        mu = jnp.mean(hj, axis=-1, keepdims=True)
        var = jnp.mean(jnp.square(hj - mu), axis=-1, keepdims=True)
        hn = (hj - mu) * lax.rsqrt(var + EPS)
        out = (hn * gn_ref[:, sl] + skip_ref[:, sl] * xc[:, sl]) * jax.nn.sigmoid(og_ref[:, sl])
        y_ref[:, sl] = out.astype(BF16)


def _mlstm(xm, vm, og, conv_w, conv_b, wq, wk, wgt, bgt, gn_g, skip, batch):
    t = xm.shape[0]
    nc = t // batch // B_CHUNK
    row = pl.BlockSpec((B_CHUNK, B_WIDTH), lambda b, c: (b * nc + c, 0))
    return pl.pallas_call(
        _mlstm_body,
        grid=(batch, nc),
        in_specs=[row, row, row, _full(conv_w.shape), _full(conv_b.shape), _full(wq.shape), _full(wk.shape),
                  _full(wgt.shape), _full(bgt.shape), _full(gn_g.shape), _full(skip.shape)],
        out_specs=row,
        out_shape=jax.ShapeDtypeStruct((t, B_WIDTH), BF16),
        scratch_shapes=[pltpu.VMEM((8 + B_CHUNK, B_WIDTH), F32),
                        pltpu.VMEM((B_CHUNK, 3 * B_WIDTH), BF16),
                        pltpu.VMEM((B_HEADS, B_HEAD_DIM, 2 * LANES), F32),
                        pltpu.VMEM((8, LANES), F32)],
        compiler_params=_params("parallel", "arbitrary"),
        name="mlstm",
    )(xm, vm, og, conv_w, conv_b, wq, wk, wgt, bgt, gn_g, skip)


C_Z = C_Q_LORA + C_KV_LORA + IDX_HEADS * LANES + LANES + LANES


def _dsa_prep_body(x_ref, g_ref, w_in_ref, cq_g_ref, ckv_g_ref, w_uq_ref, w_ukt_ref,
                   qlat_ref, ckv_ref, qi_ref, ki_ref, wi_ref):
    h = _rms(x_ref[...], g_ref[...]).astype(BF16)
    z = _dot(h, w_in_ref[...])
    o = 0
    cq = z[:, o:o + C_Q_LORA]; o += C_Q_LORA
    ckv = z[:, o:o + C_KV_LORA]; o += C_KV_LORA
    qi_ref[...] = z[:, o:o + IDX_HEADS * LANES].astype(BF16); o += IDX_HEADS * LANES
    ki_ref[...] = z[:, o:o + LANES].astype(BF16); o += LANES
    wi_ref[...] = z[:, o:o + LANES]
    ckv_ref[...] = _rms(ckv, ckv_g_ref[...]).astype(BF16)
    q = _dot(_rms(cq, cq_g_ref[...]).astype(BF16), w_uq_ref[...]).astype(BF16)
    scale = C_NOPE ** -0.5
    for hd in range(C_HEADS):
        ql = _dot(q[:, hd * C_NOPE:(hd + 1) * C_NOPE], w_ukt_ref[hd]) * scale
        qlat_ref[:, hd * C_KV_LORA:(hd + 1) * C_KV_LORA] = ql.astype(BF16)


def _dsa_prep(x, g, w_in, cq_g, ckv_g, w_uq, w_ukt, *, tm):
    t, d = x.shape
    row = lambda w: pl.BlockSpec((tm, w), lambda i: (i, 0))
    return pl.pallas_call(
        _dsa_prep_body,
        grid=(t // tm,),
        in_specs=[row(d), _full((1, d)), _full(w_in.shape), _full(cq_g.shape), _full(ckv_g.shape),
                  _full(w_uq.shape), _full(w_ukt.shape)],
        out_specs=[row(C_HEADS * C_KV_LORA), row(C_KV_LORA), row(IDX_HEADS * LANES), row(LANES), row(LANES)],
        out_shape=[jax.ShapeDtypeStruct((t, C_HEADS * C_KV_LORA), BF16),
                   jax.ShapeDtypeStruct((t, C_KV_LORA), BF16),
                   jax.ShapeDtypeStruct((t, IDX_HEADS * LANES), BF16),
                   jax.ShapeDtypeStruct((t, LANES), BF16),
                   jax.ShapeDtypeStruct((t, LANES), F32)],
        compiler_params=_params("parallel"),
        name="dsa_prep",
    )(x, g, w_in, cq_g, ckv_g, w_uq, w_ukt)


def _key_to_f32(key):
    return pltpu.bitcast(key ^ ((key >> 31) & 0x7FFFFFFF), F32)


def _dsa_attn_body(topk, qi_ref, wi_ref, qlat_ref, ki_ref, ckv_ref, w_uv_ref, o_ref,
                   score_scr, m_scr, l_scr, acc_scr):
    i = pl.program_id(1)
    Q = Q_BLOCK
    H = C_HEADS
    n_chunks = i + 1
    end = n_chunks * Q
    kk = jnp.minimum(topk, end)
    row_pos = i * Q + lax.broadcasted_iota(jnp.int32, (Q, Q), 0)
    col_iota = lax.broadcasted_iota(jnp.int32, (Q, Q), 1)

    qi = jnp.concatenate([qi_ref[:, h * LANES:(h + 1) * LANES] for h in range(IDX_HEADS)], axis=0)
    wi = wi_ref[...]

    def score_chunk(c, _):
        start = pl.multiple_of(c * Q, Q)
        rel = jnp.maximum(_dot_nt(qi, ki_ref[pl.ds(start, Q), :]), 0.0)
        score = rel[0:Q] * wi[:, 0:1]
        for h in range(1, IDX_HEADS):
            score += rel[h * Q:(h + 1) * Q] * wi[:, h:h + 1]
        score = jnp.where(start + col_iota <= row_pos, score, -jnp.inf)
        score_scr[:, pl.ds(start, Q)] = score
        return 0

    lax.fori_loop(0, n_chunks, score_chunk, 0)

    ones = jnp.ones((Q, LANES), BF16)

    def count(pred_fn):
        def body(c, acc):
            start = pl.multiple_of(c * Q, Q)
            hit = jnp.where(pred_fn(score_scr[:, pl.ds(start, Q)], start), 1.0, 0.0).astype(BF16)
            return acc + _dot(hit, ones)
        return lax.fori_loop(0, n_chunks, body, jnp.zeros((Q, LANES), F32))

    kkf = kk.astype(F32)

    def bit_step(b, key):
        bit = jnp.left_shift(jnp.int32(1), 31 - b)
        cand = jnp.where(b == 0, jnp.zeros_like(key), key | bit)
        cand_f = _key_to_f32(cand)
        cnt = count(lambda sc, start: sc >= cand_f)
        return jnp.where(cnt >= kkf, cand, key)

    key = lax.fori_loop(0, 32, bit_step, jnp.full((Q, LANES), INT_MIN, jnp.int32))
    thr = _key_to_f32(jnp.maximum(key, KEY_LOWEST_FINITE))
    n_above = count(lambda sc, start: sc > thr)
    n_tied = count(lambda sc, start: sc == thr)
    need = kkf - n_above

    def pos_step(b, bound):
        cand = bound - jnp.right_shift(jnp.int32(1 << 30), b)
        cnt = count(lambda sc, start: (sc == thr) & (start + col_iota <= cand))
        return jnp.where(cnt >= need, cand, bound)

    big = jnp.int32((1 << 31) - 1)
    bound = lax.cond(
        jnp.max(n_tied - need) > 0.0,
        lambda: lax.fori_loop(0, 31, pos_step, jnp.full((Q, LANES), big, jnp.int32)),
        lambda: jnp.full((Q, LANES), big, jnp.int32))

    m_scr[...] = jnp.full_like(m_scr, -jnp.inf)
    l_scr[...] = jnp.zeros_like(l_scr)
    acc_scr[...] = jnp.zeros_like(acc_scr)
    qlat = jnp.concatenate([qlat_ref[:, h * C_KV_LORA:(h + 1) * C_KV_LORA] for h in range(H)], axis=0)

    def attn_chunk(c, _):
        start = pl.multiple_of(c * Q, Q)
        sc = score_scr[:, pl.ds(start, Q)]
        sel = (sc > thr) | ((sc == thr) & (start + col_iota <= bound))
        ckv = ckv_ref[pl.ds(start, Q), :]
        logits = _dot_nt(qlat, ckv)
        for h in range(H):
            rs = slice(h * Q, (h + 1) * Q)
            s = jnp.where(sel, logits[rs], -jnp.inf)
            m_old = m_scr[rs]
            m_new = jnp.maximum(m_old, jnp.max(s, axis=-1, keepdims=True))
            m_safe = jnp.where(m_new == -jnp.inf, 0.0, m_new)
            alpha = jnp.exp(m_old - m_safe)
            p = jnp.exp(s - m_safe)
            l_scr[rs] = alpha * l_scr[rs] + jnp.sum(p, axis=-1, keepdims=True)
            acc_scr[rs] = alpha * acc_scr[rs] + _dot(p.astype(BF16), ckv)
            m_scr[rs] = m_new
        return 0

    lax.fori_loop(0, n_chunks, attn_chunk, 0)

    for h in range(H):
        rs = slice(h * Q, (h + 1) * Q)
        o_lat = (acc_scr[rs] / l_scr[rs]).astype(BF16)
        o_ref[:, h * C_V:(h + 1) * C_V] = _dot(o_lat, w_uv_ref[h]).astype(BF16)


def _dsa_attn(qi, wi, qlat, ki, ckv, w_uv, batch):
    t = qi.shape[0]
    s = t // batch
    nq = s // Q_BLOCK
    topk = min(TOPK_MAX, s // 4)
    qrow = lambda w: pl.BlockSpec((Q_BLOCK, w), lambda b, i: (b * nq + i, 0))
    seq = lambda w: pl.BlockSpec((s, w), lambda b, i: (b, 0))
    return pl.pallas_call(
        functools.partial(_dsa_attn_body, topk),
        grid=(batch, nq),
        in_specs=[qrow(IDX_HEADS * LANES), qrow(LANES), qrow(C_HEADS * C_KV_LORA), seq(LANES), seq(C_KV_LORA),
                  _full(w_uv.shape)],
        out_specs=qrow(C_HEADS * C_V),
        out_shape=jax.ShapeDtypeStruct((t, C_HEADS * C_V), BF16),
        scratch_shapes=[pltpu.VMEM((Q_BLOCK, s), F32),
                        pltpu.VMEM((C_HEADS * Q_BLOCK, 1), F32),
                        pltpu.VMEM((C_HEADS * Q_BLOCK, 1), F32),
                        pltpu.VMEM((C_HEADS * Q_BLOCK, C_KV_LORA), F32)],
        compiler_params=_params("parallel", "arbitrary"),
        name="dsa_attn",
    )(qi, wi, qlat, ki, ckv, w_uv)


def _mem_kv_body(mem_ref, g_ref, wk_ref, wv_ref, k_ref, v_ref):
    mn = _rms(mem_ref[...], g_ref[...]).astype(BF16)
    k_ref[...] = _dot(mn, wk_ref[...]).astype(BF16)
    v_ref[...] = _dot(mn, wv_ref[...]).astype(BF16)


def _mem_kv(mem, g, wk, wv, layer, n_mem):
    rows, d = mem.shape
    w = wk.shape[-1]
    wspec = pl.BlockSpec((None, d, w), lambda b: (layer, 0, 0))
    return pl.pallas_call(
        _mem_kv_body,
        grid=(rows // n_mem,),
        in_specs=[pl.BlockSpec((n_mem, d), lambda b: (b, 0)), _full((1, d)), wspec, wspec],
        out_specs=[pl.BlockSpec((n_mem, w), lambda b: (b, 0))] * 2,
        out_shape=[jax.ShapeDtypeStruct((rows, w), BF16)] * 2,
        compiler_params=_params("parallel"),
        name="mem_kv",
    )(mem, g, wk, wv)


def _xattn_body(x_ref, gpre_ref, gpost_ref, wq_ref, wo_ref, k_ref, v_ref, o_ref, att_scr):
    x = x_ref[...]
    h = _rms(x, gpre_ref[...]).astype(BF16)
    q = (_dot(h, wq_ref[...]) * X_HEAD_DIM ** -0.5).astype(BF16)
    for hd in range(X_HEADS):
        sl = slice(hd * X_HEAD_DIM, (hd + 1) * X_HEAD_DIM)
        s = _dot_nt(q[:, sl], k_ref[:, sl])
        p = jnp.exp(s - jnp.max(s, axis=-1, keepdims=True))
        o = _dot(p.astype(BF16), v_ref[:, sl]) / jnp.sum(p, axis=-1, keepdims=True)
        att_scr[:, sl] = o.astype(BF16)
    y = _dot(att_scr[...], wo_ref[...])
    o_ref[...] = x + _rms(y, gpost_ref[...])


def _xattn(x, gpre, gpost, wq, wo, k, v, layer, batch, *, tm):
    t, d = x.shape
    w = wq.shape[-1]
    n_mem = k.shape[0] // batch
    tiles_per_batch = t // batch // tm
    return pl.pallas_call(
        _xattn_body,
        grid=(t // tm,),
        in_specs=[pl.BlockSpec((tm, d), lambda i: (i, 0)), _full((1, d)), _full((1, d)),
                  pl.BlockSpec((None, d, w), lambda i: (layer, 0, 0)),
                  pl.BlockSpec((None, w, d), lambda i: (layer, 0, 0)),
                  pl.BlockSpec((n_mem, w), lambda i: (i // tiles_per_batch, 0)),
                  pl.BlockSpec((n_mem, w), lambda i: (i // tiles_per_batch, 0))],
        out_specs=pl.BlockSpec((tm, d), lambda i: (i, 0)),
        out_shape=jax.ShapeDtypeStruct((t, d), F32),
        scratch_shapes=[pltpu.VMEM((tm, w), BF16)],
        compiler_params=_params("parallel"),
        name="xattn",
    )(x, gpre, gpost, wq, wo, k, v)


def _pad_cols(w, width):
    return jnp.pad(w, ((0, 0), (0, width - w.shape[1])))


def _prep_ab_weights(w_in, w_uq, w_ukv, w_gates, b_gates):
    o = A_Q_LORA + A_KV_LORA
    w_in_p = jnp.concatenate([w_in[:, :o], _pad_cols(w_in[:, o:o + A_ROPE], LANES), w_in[:, o + A_ROPE:]], axis=1)
    uq = w_uq.reshape(A_Q_LORA, A_HEADS, A_NOPE + A_ROPE)
    uq = jnp.pad(uq, ((0, 0), (0, 0), (0, A_HEAD_PAD - A_NOPE - A_ROPE))).reshape(A_Q_LORA, A_HEADS * A_HEAD_PAD)
    ukv = w_ukv.reshape(A_KV_LORA, A_HEADS, A_NOPE + A_V)
    ukv = jnp.concatenate([ukv[:, :, :A_NOPE].reshape(A_KV_LORA, -1), ukv[:, :, A_NOPE:].reshape(A_KV_LORA, -1)], axis=1)
    return (w_in_p.astype(BF16), uq.astype(BF16), ukv.astype(BF16),
            _pad_cols(w_gates, LANES).astype(BF16), _pad_cols(b_gates[None, :], LANES))


def _prep_c_weights(w_in, w_uk):
    o = C_Q_LORA + C_KV_LORA
    qi = w_in[:, o:o + IDX_HEADS * IDX_DIM].reshape(-1, IDX_HEADS, IDX_DIM)
    qi = jnp.pad(qi, ((0, 0), (0, 0), (0, LANES - IDX_DIM))).reshape(-1, IDX_HEADS * LANES)
    o2 = o + IDX_HEADS * IDX_DIM
    w_in_p = jnp.concatenate([w_in[:, :o], qi, _pad_cols(w_in[:, o2:o2 + IDX_DIM], LANES),
                              _pad_cols(w_in[:, o2 + IDX_DIM:], LANES)], axis=1)
    return w_in_p.astype(BF16), jnp.swapaxes(w_uk, 1, 2).astype(BF16)


def kernel(x, mem, positions, norm_g, mem_norm_g, ffn_w_gate, ffn_w_up, ffn_w_down, xa_wq, xa_wk, xa_wv, xa_wo, ab_w_in, ab_w_out, mla_cq_g, mla_ckv_g, mla_w_uq, mla_w_ukv, ml_conv_w, ml_conv_b, ml_wq, ml_wk, ml_w_gates, ml_b_gates, ml_gn_g, ml_skip, c_w_in, c_w_out, c_cq_g, c_ckv_g, c_w_uq, c_w_uk, c_w_uv):
    batch, seq, d = x.shape
    depth = norm_g.shape[0]
    n_mem = mem.shape[1]
    t = batch * seq
    tm = min(512, seq)
    ffn_dim = ffn_w_gate.shape[-1]
    tf = ffn_dim // 2 if (ffn_dim // 2) % LANES == 0 else ffn_dim

    xt = x.reshape(t, d)
    memt = mem.reshape(batch * n_mem, d)
    pos = positions.reshape(t, 1).astype(F32)
    wg, wu, wd = ffn_w_gate.astype(BF16), ffn_w_up.astype(BF16), ffn_w_down.astype(BF16)
    xa_wq_b, xa_wk_b, xa_wv_b, xa_wo_b = (w.astype(BF16) for w in (xa_wq, xa_wk, xa_wv, xa_wo))

    for l in range(depth):
        g = norm_g[l][:, None, :]
        xt = _ffn(xt, g[0], g[1], wg, wu, wd, l, 0, tm=tm, tf=tf)
        if l % 2 == 0:
            e = l // 2
            w_in_p, uq, ukv, wgt, bgt = _prep_ab_weights(ab_w_in[e], mla_w_uq[e], mla_w_ukv[e],
                                                         ml_w_gates[e], ml_b_gates[e])
            q, k, v, xm, vm, og = _ab_prep(xt, pos, g[2], w_in_p, mla_cq_g[e][None], mla_ckv_g[e][None], uq, ukv, tm=tm)
            ya = _mla_attn(q, k, v, batch, tq=min(256, seq))
            yb = _mlstm(xm, vm, og, ml_conv_w[e], ml_conv_b[e][None], ml_wq[e].astype(BF16), ml_wk[e].astype(BF16),
                        wgt, bgt, ml_gn_g[e][None], ml_skip[e][None], batch)
            w_out = ab_w_out[e].astype(BF16)
            xt = _proj(xt, g[3], [ya, yb], [w_out[:A_HEADS * A_V], w_out[A_HEADS * A_V:]], tm=tm)
        else:
            o = l // 2
            w_in_p, w_ukt = _prep_c_weights(c_w_in[o], c_w_uk[o])
            qlat, ckv, qi, ki, wi = _dsa_prep(xt, g[2], w_in_p, c_cq_g[o][None], c_ckv_g[o][None],
                                              c_w_uq[o].astype(BF16), w_ukt, tm=tm)
            yc = _dsa_attn(qi, wi, qlat, ki, ckv, c_w_uv[o].astype(BF16), batch)
            xt = _proj(xt, g[3], [yc], [c_w_out[o].astype(BF16)], tm=tm)
        mk, mv = _mem_kv(memt, mem_norm_g[l][None], xa_wk_b, xa_wv_b, l, n_mem)
        xt = _xattn(xt, g[4], g[5], xa_wq_b, xa_wo_b, mk, mv, l, batch, tm=tm)
        xt = _ffn(xt, g[6], g[7], wg, wu, wd, l, 1, tm=tm, tf=tf)
    return xt.reshape(batch, seq, d)
```

```python
import functools
import math

import jax
import jax.numpy as jnp
from jax import lax
from jax.experimental import pallas as pl
from jax.experimental.pallas import tpu as pltpu

F32 = jnp.float32
BF16 = jnp.bfloat16

EPS = 1e-6
ROPE_BASE = 10000.0
LANES = 128
VMEM_LIMIT = 48 * 1024 * 1024

A_HEADS, A_Q_LORA, A_KV_LORA, A_NOPE, A_ROPE, A_V = 4, 384, 256, 128, 64, 128
B_HEADS, B_HEAD_DIM, B_CONV, B_CHUNK = 4, 128, 4, 128
B_WIDTH = B_HEADS * B_HEAD_DIM
C_HEADS, C_Q_LORA, C_KV_LORA, C_NOPE, C_V = 8, 384, 256, 128, 128
IDX_HEADS, IDX_DIM, TOPK_MAX = 8, 64, 256
X_HEADS, X_HEAD_DIM = 4, 128
Q_BLOCK = 128

INT_MIN = -(2 ** 31)
KEY_LOWEST_FINITE = INT_MIN + 0x00800000


def _params(*sem):
    return pltpu.CompilerParams(dimension_semantics=sem, vmem_limit_bytes=VMEM_LIMIT)


def _rms(x, g):
    return x * lax.rsqrt(jnp.mean(x * x, axis=-1, keepdims=True) + EPS) * g


def _dot(a, b):
    return jnp.dot(a, b, preferred_element_type=F32)


def _dot_nt(a, b):
    return lax.dot_general(a, b, (((1,), (1,)), ((), ())), preferred_element_type=F32)


def _dot_tn(a, b):
    return lax.dot_general(a, b, (((0,), (0,)), ((), ())), preferred_element_type=F32)


def _full(shape):
    return pl.BlockSpec(shape, lambda *_: (0,) * len(shape))


def _ffn_body(x_ref, gpre_ref, gpost_ref, wg_ref, wu_ref, wd_ref, o_ref, h_scr, acc_scr):
    j = pl.program_id(1)

    @pl.when(j == 0)
    def _():
        h_scr[...] = _rms(x_ref[...], gpre_ref[...]).astype(BF16)
        acc_scr[...] = jnp.zeros_like(acc_scr)

    h = h_scr[...]
    g = _dot(h, wg_ref[...])
    u = _dot(h, wu_ref[...])
    a = (g * jax.nn.sigmoid(g) * u).astype(BF16)
    acc_scr[...] += _dot(a, wd_ref[...])

    @pl.when(j == pl.num_programs(1) - 1)
    def _():
        o_ref[...] = x_ref[...] + 0.5 * _rms(acc_scr[...], gpost_ref[...])


def _ffn(x, gpre, gpost, wg, wu, wd, layer, slot, *, tm, tf):
    t, d = x.shape
    f = wg.shape[-1]
    return pl.pallas_call(
        _ffn_body,
        grid=(t // tm, f // tf),
        in_specs=[
            pl.BlockSpec((tm, d), lambda i, j: (i, 0)),
            _full((1, d)), _full((1, d)),
            pl.BlockSpec((None, None, d, tf), lambda i, j: (layer, slot, 0, j)),
            pl.BlockSpec((None, None, d, tf), lambda i, j: (layer, slot, 0, j)),
            pl.BlockSpec((None, None, tf, d), lambda i, j: (layer, slot, j, 0)),
        ],
        out_specs=pl.BlockSpec((tm, d), lambda i, j: (i, 0)),
        out_shape=jax.ShapeDtypeStruct((t, d), F32),
        scratch_shapes=[pltpu.VMEM((tm, d), BF16), pltpu.VMEM((tm, d), F32)],
        compiler_params=_params("parallel", "arbitrary"),
        name="ffn",
    )(x, gpre, gpost, wg, wu, wd)


def _proj_body(n_in, x_ref, g_ref, *refs):
    a_refs, w_refs, o_ref = refs[:n_in], refs[n_in:2 * n_in], refs[2 * n_in]
    y = _dot(a_refs[0][...], w_refs[0][...])
    for a_ref, w_ref in zip(a_refs[1:], w_refs[1:]):
        y += _dot(a_ref[...], w_ref[...])
    o_ref[...] = x_ref[...] + _rms(y, g_ref[...])


def _proj(x, g, acts, weights, *, tm):
    t, d = x.shape
    n_in = len(acts)
    return pl.pallas_call(
        functools.partial(_proj_body, n_in),
        grid=(t // tm,),
        in_specs=[pl.BlockSpec((tm, d), lambda i: (i, 0)), _full((1, d))]
        + [pl.BlockSpec((tm, a.shape[1]), lambda i: (i, 0)) for a in acts]
        + [_full(w.shape) for w in weights],
        out_specs=pl.BlockSpec((tm, d), lambda i: (i, 0)),
        out_shape=jax.ShapeDtypeStruct((t, d), F32),
        compiler_params=_params("parallel"),
        name="mixer_out_proj",
    )(x, g, *acts, *weights)


def _rope_tables(pos):
    half = A_ROPE // 2
    lane = lax.broadcasted_iota(jnp.int32, (1, LANES), 1)
    j = (lane % half).astype(F32)
    inv = jnp.exp(j * (-math.log(ROPE_BASE) / half))
    ang = pos * inv
    sin = jnp.where(lane < A_ROPE, jnp.sin(ang), 0.0)
    return jnp.cos(ang), sin


def _rope_apply(r, cos, sin):
    half = A_ROPE // 2
    return r * cos + (pltpu.roll(r, half, axis=1) - pltpu.roll(r, LANES - half, axis=1)) * sin


AB_Z = A_Q_LORA + A_KV_LORA + LANES + 3 * B_WIDTH
A_HEAD_PAD = 2 * LANES


def _ab_prep_body(x_ref, pos_ref, g_ref, w_in_ref, cq_g_ref, ckv_g_ref, w_uq_ref, w_ukv_ref,
                  q_ref, k_ref, v_ref, xm_ref, vm_ref, og_ref):
    h = _rms(x_ref[...], g_ref[...]).astype(BF16)
    z = _dot(h, w_in_ref[...])
    o = 0
    cq = z[:, o:o + A_Q_LORA]; o += A_Q_LORA
    ckv = z[:, o:o + A_KV_LORA]; o += A_KV_LORA
    kr = z[:, o:o + LANES]; o += LANES
    xm_ref[...] = z[:, o:o + B_WIDTH]; o += B_WIDTH
    vm_ref[...] = z[:, o:o + B_WIDTH]; o += B_WIDTH
    og_ref[...] = z[:, o:o + B_WIDTH]

    cos, sin = _rope_tables(pos_ref[...])
    scale = (A_NOPE + A_ROPE) ** -0.5
    q = _dot(_rms(cq, cq_g_ref[...]).astype(BF16), w_uq_ref[...]) * scale
    kv = _dot(_rms(ckv, ckv_g_ref[...]).astype(BF16), w_ukv_ref[...])
    k_rope = _rope_apply(kr, cos, sin).astype(BF16)
    for hd in range(A_HEADS):
        base = hd * A_HEAD_PAD
        q_ref[:, base:base + A_NOPE] = q[:, base:base + A_NOPE].astype(BF16)
        q_ref[:, base + A_NOPE:base + A_HEAD_PAD] = _rope_apply(
            q[:, base + A_NOPE:base + A_HEAD_PAD], cos, sin).astype(BF16)
        k_ref[:, base:base + A_NOPE] = kv[:, hd * A_NOPE:(hd + 1) * A_NOPE].astype(BF16)
        k_ref[:, base + A_NOPE:base + A_HEAD_PAD] = k_rope
    v_ref[...] = kv[:, A_HEADS * A_NOPE:].astype(BF16)


def _ab_prep(x, pos, g, w_in, cq_g, ckv_g, w_uq, w_ukv, *, tm):
    t, d = x.shape
    row = lambda w: pl.BlockSpec((tm, w), lambda i: (i, 0))
    return pl.pallas_call(
        _ab_prep_body,
        grid=(t // tm,),
        in_specs=[row(d), row(1), _full((1, d)), _full(w_in.shape), _full(cq_g.shape), _full(ckv_g.shape),
                  _full(w_uq.shape), _full(w_ukv.shape)],
        out_specs=[row(A_HEADS * A_HEAD_PAD), row(A_HEADS * A_HEAD_PAD), row(A_HEADS * A_V),
                   row(B_WIDTH), row(B_WIDTH), row(B_WIDTH)],
        out_shape=[jax.ShapeDtypeStruct((t, A_HEADS * A_HEAD_PAD), BF16),
                   jax.ShapeDtypeStruct((t, A_HEADS * A_HEAD_PAD), BF16),
                   jax.ShapeDtypeStruct((t, A_HEADS * A_V), BF16),
                   jax.ShapeDtypeStruct((t, B_WIDTH), F32),
                   jax.ShapeDtypeStruct((t, B_WIDTH), F32),
                   jax.ShapeDtypeStruct((t, B_WIDTH), F32)],
        compiler_params=_params("parallel"),
        name="ab_prep",
    )(x, pos, g, w_in, cq_g, ckv_g, w_uq, w_ukv)


def _mla_attn_body(tq, q_ref, k_ref, v_ref, o_ref, acc_scr):
    i = pl.program_id(1)
    heads = range(A_HEADS)
    acc_scr[...] = jnp.zeros(acc_scr.shape, F32)

    def chunk(j, carry, masked):
        ms, ls = carry
        start = pl.multiple_of(j * tq, tq)
        new_ms, new_ls = [], []
        for h in heads:
            s = _dot_nt(q_ref[:, h * A_HEAD_PAD:(h + 1) * A_HEAD_PAD],
                        k_ref[pl.ds(start, tq), h * A_HEAD_PAD:(h + 1) * A_HEAD_PAD])
            if masked:
                row = lax.broadcasted_iota(jnp.int32, (tq, tq), 0)
                col = lax.broadcasted_iota(jnp.int32, (tq, tq), 1)
                s = jnp.where(col <= row, s, -jnp.inf)
            m_new = jnp.maximum(ms[h], jnp.max(s, axis=-1, keepdims=True))
            alpha = jnp.exp(ms[h] - m_new)
            p = jnp.exp(s - m_new)
            new_ls.append(alpha * ls[h] + jnp.sum(p, axis=-1, keepdims=True))
            acc_scr[h] = alpha * acc_scr[h] + _dot(p.astype(BF16), v_ref[pl.ds(start, tq), h * A_V:(h + 1) * A_V])
            new_ms.append(m_new)
        return tuple(new_ms), tuple(new_ls)

    init = (tuple(jnp.full((tq, 1), -jnp.inf, F32) for _ in heads), tuple(jnp.zeros((tq, 1), F32) for _ in heads))
    carry = lax.fori_loop(0, i, lambda j, c: chunk(j, c, False), init)
    _, ls = chunk(i, carry, True)
    for h in heads:
        o_ref[:, h * A_V:(h + 1) * A_V] = (acc_scr[h] / ls[h]).astype(BF16)


def _mla_attn(q, k, v, batch, *, tq):
    t = q.shape[0]
    s = t // batch
    nq = s // tq
    return pl.pallas_call(
        functools.partial(_mla_attn_body, tq),
        grid=(batch, nq),
        in_specs=[pl.BlockSpec((tq, A_HEADS * A_HEAD_PAD), lambda b, i: (b * nq + i, 0)),
                  pl.BlockSpec((s, A_HEADS * A_HEAD_PAD), lambda b, i: (b, 0)),
                  pl.BlockSpec((s, A_HEADS * A_V), lambda b, i: (b, 0))],
        out_specs=pl.BlockSpec((tq, A_HEADS * A_V), lambda b, i: (b * nq + i, 0)),
        out_shape=jax.ShapeDtypeStruct((t, A_HEADS * A_V), BF16),
        scratch_shapes=[pltpu.VMEM((A_HEADS, tq, A_V), F32)],
        compiler_params=_params("parallel", "arbitrary"),
        name="mla_attn",
    )(q, k, v)


def _log_sigmoid(x):
    return jnp.minimum(x, 0.0) - jnp.log(1.0 + jnp.exp(-jnp.abs(x)))


def _mlstm_body(xm_ref, vm_ref, og_ref, conv_w_ref, conv_b_ref, wq_ref, wk_ref, wgt_ref, bgt_ref,
                gn_ref, skip_ref, y_ref, xext_scr, gin_scr, state_scr, m_scr):
    c = pl.program_id(1)
    L = B_CHUNK
    tail = 8

    @pl.when(c == 0)
    def _():
        xext_scr[0:tail, :] = jnp.zeros((tail, B_WIDTH), F32)
        state_scr[...] = jnp.zeros_like(state_scr)
        m_scr[...] = jnp.zeros_like(m_scr)

    xm = xm_ref[...]
    xext_scr[tail:tail + L, :] = xm
    conv = conv_b_ref[...] + conv_w_ref[B_CONV - 1:B_CONV, :] * xm
    for tap in range(B_CONV - 1):
        shift = B_CONV - 1 - tap
        conv += conv_w_ref[tap:tap + 1, :] * xext_scr[tail - shift:tail - shift + L, :]
    xext_scr[0:tail, :] = xm[L - tail:, :]
    xc = conv * jax.nn.sigmoid(conv)
    xcb = xc.astype(BF16)

    vm = vm_ref[...]
    qs, ks = [], []
    for h in range(B_HEADS):
        sl = slice(h * B_HEAD_DIM, (h + 1) * B_HEAD_DIM)
        qh = _dot(xcb[:, sl], wq_ref[h])
        kh = _dot(xcb[:, sl], wk_ref[h])
        qs.append(qh)
        ks.append(kh)
        gin_scr[:, h * B_HEAD_DIM:(h + 1) * B_HEAD_DIM] = qh.astype(BF16)
        gin_scr[:, B_WIDTH + h * B_HEAD_DIM:B_WIDTH + (h + 1) * B_HEAD_DIM] = kh.astype(BF16)
    gin_scr[:, 2 * B_WIDTH:] = vm.astype(BF16)
    gates = _dot(gin_scr[...], wgt_ref[...]) + bgt_ref[...]

    lane = lax.broadcasted_iota(jnp.int32, (L, LANES), 1)
    row = lax.broadcasted_iota(jnp.int32, (L, LANES), 0)
    is_f = (lane >= B_HEADS) & (lane < 2 * B_HEADS)
    glog = jnp.where(is_f, _log_sigmoid(gates), gates)
    bcum = jnp.where(is_f, glog, 0.0)
    step = 1
    while step < L:
        bcum = bcum + jnp.where(row >= step, pltpu.roll(bcum, step, axis=0), 0.0)
        step *= 2
    glog_t = glog.T
    bcum_t = bcum.T

    causal = lax.broadcasted_iota(jnp.int32, (L, L), 1) <= lax.broadcasted_iota(jnp.int32, (L, L), 0)
    ones_col = jnp.where(lax.broadcasted_iota(jnp.int32, (L, LANES), 1) == 0, 1.0, 0.0).astype(BF16)
    kscale = B_HEAD_DIM ** -0.5
    for h in range(B_HEADS):
        sl = slice(h * B_HEAD_DIM, (h + 1) * B_HEAD_DIM)
        qb = qs[h].astype(BF16)
        k_s = ks[h] * kscale
        v_ext = jnp.concatenate([vm[:, sl].astype(BF16), ones_col], axis=1)
        li_c = glog[:, h:h + 1]
        b_c = bcum[:, B_HEADS + h:B_HEADS + h + 1]
        li_r = glog_t[h:h + 1, :]
        b_r = bcum_t[B_HEADS + h:B_HEADS + h + 1, :]
        b_tot = b_r[:, L - 1:L]
        m_st = m_scr[h:h + 1, 0:1]
        state = state_scr[h]

        dmat = jnp.where(causal, b_c - b_r + li_r, -jnp.inf)
        m_inter = b_c + m_st
        m_j = jnp.maximum(m_inter, jnp.max(dmat, axis=-1, keepdims=True))
        sc = _dot_nt(qb, k_s.astype(BF16)) * jnp.exp(dmat - m_j)
        inter = jnp.exp(m_inter - m_j)
        ne = inter * _dot(qb, state.astype(BF16)) + _dot(sc.astype(BF16), v_ext)
        den = ne[:, B_HEAD_DIM:B_HEAD_DIM + 1]
        hj = ne[:, :B_HEAD_DIM] / jnp.maximum(jnp.abs(den), jnp.exp(-m_j))

        g_c = b_tot - b_c + li_c
        g_r = b_tot - b_r + li_r
        m_new = jnp.maximum(b_tot + m_st, jnp.max(g_r, axis=-1, keepdims=True))
        decay = jnp.exp(b_tot + m_st - m_new)
        kw = (k_s * jnp.exp(g_c - m_new)).astype(BF16)
        state_scr[h] = decay * state + _dot_tn(kw, v_ext)
        m_scr[h:h + 1, :] = jnp.broadcast_to(m_new, (1, LANES))

        mu = jnp.mean(hj, axis=-1, keepdims=True)
        var = jnp.mean(jnp.square(hj - mu), axis=-1, keepdims=True)
        hn = (hj - mu) * lax.rsqrt(var + EPS)
        out = (hn * gn_ref[:, sl] + skip_ref[:, sl] * xc[:, sl]) * jax.nn.sigmoid(og_ref[:, sl])
        y_ref[:, sl] = out.astype(BF16)


def _mlstm(xm, vm, og, conv_w, conv_b, wq, wk, wgt, bgt, gn_g, skip, batch):
    t = xm.shape[0]
    nc = t // batch // B_CHUNK
    row = pl.BlockSpec((B_CHUNK, B_WIDTH), lambda b, c: (b * nc + c, 0))
    return pl.pallas_call(
        _mlstm_body,
        grid=(batch, nc),
        in_specs=[row, row, row, _full(conv_w.shape), _full(conv_b.shape), _full(wq.shape), _full(wk.shape),
                  _full(wgt.shape), _full(bgt.shape), _full(gn_g.shape), _full(skip.shape)],
        out_specs=row,
        out_shape=jax.ShapeDtypeStruct((t, B_WIDTH), BF16),
        scratch_shapes=[pltpu.VMEM((8 + B_CHUNK, B_WIDTH), F32),
                        pltpu.VMEM((B_CHUNK, 3 * B_WIDTH), BF16),
                        pltpu.VMEM((B_HEADS, B_HEAD_DIM, 2 * LANES), F32),
                        pltpu.VMEM((8, LANES), F32)],
        compiler_params=_params("parallel", "arbitrary"),
        name="mlstm",
    )(xm, vm, og, conv_w, conv_b, wq, wk, wgt, bgt, gn_g, skip)


C_Z = C_Q_LORA + C_KV_LORA + IDX_HEADS * LANES + LANES + LANES


def _dsa_prep_body(x_ref, g_ref, w_in_ref, cq_g_ref, ckv_g_ref, w_uq_ref, w_ukt_ref,
                   qlat_ref, ckv_ref, qi_ref, ki_ref, wi_ref):
    h = _rms(x_ref[...], g_ref[...]).astype(BF16)
    z = _dot(h, w_in_ref[...])
    o = 0
    cq = z[:, o:o + C_Q_LORA]; o += C_Q_LORA
    ckv = z[:, o:o + C_KV_LORA]; o += C_KV_LORA
    qi_ref[...] = z[:, o:o + IDX_HEADS * LANES].astype(BF16); o += IDX_HEADS * LANES
    ki_ref[...] = z[:, o:o + LANES].astype(BF16); o += LANES
    wi_ref[...] = z[:, o:o + LANES]
    ckv_ref[...] = _rms(ckv, ckv_g_ref[...]).astype(BF16)
    q = _dot(_rms(cq, cq_g_ref[...]).astype(BF16), w_uq_ref[...]).astype(BF16)
    scale = C_NOPE ** -0.5
    for hd in range(C_HEADS):
        ql = _dot(q[:, hd * C_NOPE:(hd + 1) * C_NOPE], w_ukt_ref[hd]) * scale
        qlat_ref[:, hd * C_KV_LORA:(hd + 1) * C_KV_LORA] = ql.astype(BF16)


def _dsa_prep(x, g, w_in, cq_g, ckv_g, w_uq, w_ukt, *, tm):
    t, d = x.shape
    row = lambda w: pl.BlockSpec((tm, w), lambda i: (i, 0))
    return pl.pallas_call(
        _dsa_prep_body,
        grid=(t // tm,),
        in_specs=[row(d), _full((1, d)), _full(w_in.shape), _full(cq_g.shape), _full(ckv_g.shape),
                  _full(w_uq.shape), _full(w_ukt.shape)],
        out_specs=[row(C_HEADS * C_KV_LORA), row(C_KV_LORA), row(IDX_HEADS * LANES), row(LANES), row(LANES)],
        out_shape=[jax.ShapeDtypeStruct((t, C_HEADS * C_KV_LORA), BF16),
                   jax.ShapeDtypeStruct((t, C_KV_LORA), BF16),
                   jax.ShapeDtypeStruct((t, IDX_HEADS * LANES), BF16),
                   jax.ShapeDtypeStruct((t, LANES), BF16),
                   jax.ShapeDtypeStruct((t, LANES), F32)],
        compiler_params=_params("parallel"),
        name="dsa_prep",
    )(x, g, w_in, cq_g, ckv_g, w_uq, w_ukt)


INT_MAX = 2 ** 31 - 1
DSA_VMEM_LIMIT = 56 * 1024 * 1024


def _key_to_f32(key):
    return pltpu.bitcast(key ^ ((key >> 31) & 0x7FFFFFFF), F32)


def _dsa_attn_body(topk, seq, qi_ref, wi_ref, qlat_ref, ki_ref, ckv_ref, w_uv_ref, o_ref,
                   score_scr, key_scr, cnt_scr, bound_scr, m_scr, l_scr, acc_scr):
    Q = Q_BLOCK
    H = C_HEADS
    nb = seq // Q
    first = topk // Q
    col_iota = lax.broadcasted_iota(jnp.int32, (Q, Q), 1)
    ones = jnp.ones((Q, LANES), BF16)
    topk_f = float(topk)

    qa = 2 * Q
    qa_col = lax.broadcasted_iota(jnp.int32, (qa, qa), 1)
    qa_row = lax.broadcasted_iota(jnp.int32, (qa, qa), 0)

    def score_block(blk, _):
        r0 = pl.multiple_of(blk * qa, qa)
        qi = jnp.concatenate([qi_ref[pl.ds(r0, qa), h * LANES:(h + 1) * LANES] for h in range(IDX_HEADS)], axis=0)
        wi = wi_ref[pl.ds(r0, qa), :]
        wb = [jnp.broadcast_to(wi[:, h:h + 1], (qa, qa)) for h in range(IDX_HEADS)]

        def tile(c, _):
            k0 = pl.multiple_of(c * qa, qa)
            rel = jnp.maximum(_dot_nt(qi, ki_ref[pl.ds(k0, qa), :]), 0.0)
            score = rel[0:qa] * wb[0]
            for h in range(1, IDX_HEADS):
                score += rel[h * qa:(h + 1) * qa] * wb[h]
            score = jnp.where(k0 + qa_col <= r0 + qa_row, score, -jnp.inf)
            score_scr[pl.ds(r0, qa), pl.ds(k0, qa)] = score
            return 0

        lax.fori_loop(0, blk + 1, tile, 0)
        return 0

    lax.fori_loop(0, seq // qa, score_block, 0)

    if first > 0:
        key_scr[0:first * Q, :] = jnp.full((first * Q, LANES), KEY_LOWEST_FINITE, jnp.int32)
        cnt_scr[0:first * Q, :] = jnp.full((first * Q, LANES), topk_f, F32)
    for i in range(first, nb):
        key_scr[i * Q:(i + 1) * Q, :] = jnp.full((Q, LANES), INT_MIN, jnp.int32)
        cnt_scr[i * Q:(i + 1) * Q, :] = jnp.full((Q, LANES), float((i + 1) * Q), F32)

    def bit_step(b, _):
        bit = jnp.left_shift(jnp.int32(1), 31 - b)
        for i in range(first, nb):
            rows = slice(i * Q, (i + 1) * Q)
            key = key_scr[rows, :]
            cand = jnp.where(b == 0, jnp.zeros_like(key), key | bit)
            cand_f = _key_to_f32(cand)
            hits = jnp.zeros((Q, LANES), F32)
            for c in range(i + 1):
                hits += jnp.where(score_scr[rows, c * Q:(c + 1) * Q] >= cand_f, 1.0, 0.0)
            cnt = _dot(hits.astype(BF16), ones)
            ok = cnt >= topk_f
            key_scr[rows, :] = jnp.where(ok, cand, key)
            cnt_scr[rows, :] = jnp.where(ok, cnt, cnt_scr[rows, :])
        return 0

    lax.fori_loop(0, 32, bit_step, 0)

    bound_scr[...] = jnp.full(bound_scr.shape, INT_MAX, jnp.int32)

    @pl.when(jnp.max(cnt_scr[...]) > topk_f)
    def _():
        def tie_block(i, _):
            r0 = pl.multiple_of(i * Q, Q)
            thr = _key_to_f32(key_scr[pl.ds(r0, Q), :])

            def count(pred_fn):
                def body(c, hits):
                    k0 = pl.multiple_of(c * Q, Q)
                    return hits + jnp.where(pred_fn(score_scr[pl.ds(r0, Q), pl.ds(k0, Q)], k0), 1.0, 0.0)
                return _dot(lax.fori_loop(0, i + 1, body, jnp.zeros((Q, LANES), F32)).astype(BF16), ones)

            need = topk_f - count(lambda sc, k0: sc > thr)

            def pos_step(b, bound):
                cand = bound - jnp.right_shift(jnp.int32(1 << 30), b)
                cnt = count(lambda sc, k0: (sc == thr) & (k0 + col_iota <= cand))
                return jnp.where(cnt >= need, cand, bound)

            bound_scr[pl.ds(r0, Q), :] = lax.fori_loop(0, 31, pos_step, jnp.full((Q, LANES), INT_MAX, jnp.int32))
            return 0

        lax.fori_loop(first, nb, tie_block, 0)

    twice = lambda a: jnp.concatenate([a, a], axis=1)

    def attn_block(blk, _):
        r0 = pl.multiple_of(blk * qa, qa)
        qlat = jnp.concatenate([qlat_ref[pl.ds(r0, qa), h * C_KV_LORA:(h + 1) * C_KV_LORA] for h in range(H)], axis=0)
        thr = twice(_key_to_f32(key_scr[pl.ds(r0, qa), :]))
        bound = twice(bound_scr[pl.ds(r0, qa), :])
        m_scr[...] = jnp.full(m_scr.shape, -jnp.inf, F32)
        l_scr[...] = jnp.zeros(l_scr.shape, F32)
        acc_scr[...] = jnp.zeros(acc_scr.shape, F32)

        def tile(c, _):
            k0 = pl.multiple_of(c * qa, qa)
            sc = score_scr[pl.ds(r0, qa), pl.ds(k0, qa)]
            sel = (sc > thr) | ((sc == thr) & (k0 + qa_col <= bound))
            ckv = ckv_ref[pl.ds(k0, qa), :]
            logits = _dot_nt(qlat, ckv)
            for h in range(H):
                rs = slice(h * qa, (h + 1) * qa)
                s = jnp.where(sel, logits[rs], -jnp.inf)
                m_old = m_scr[rs]
                m_new = jnp.maximum(m_old, jnp.max(s, axis=-1, keepdims=True))
                m_safe = jnp.where(m_new == -jnp.inf, 0.0, m_new)
                alpha = jnp.exp(m_old - m_safe)
                p = jnp.exp(s - twice(m_safe))
                l_scr[rs] = alpha * l_scr[rs] + jnp.sum(p, axis=-1, keepdims=True)
                acc_scr[rs] = twice(alpha) * acc_scr[rs] + _dot(p.astype(BF16), ckv)
                m_scr[rs] = m_new
            return 0

        lax.fori_loop(0, blk + 1, tile, 0)

        for h in range(H):
            rs = slice(h * qa, (h + 1) * qa)
            o_lat = (acc_scr[rs] / twice(l_scr[rs])).astype(BF16)
            o_ref[pl.ds(r0, qa), h * C_V:(h + 1) * C_V] = _dot(o_lat, w_uv_ref[h]).astype(BF16)
        return 0

    lax.fori_loop(0, seq // qa, attn_block, 0)


def _dsa_attn(qi, wi, qlat, ki, ckv, w_uv, batch):
    t = qi.shape[0]
    s = t // batch
    topk = min(TOPK_MAX, s // 4)
    assert topk % Q_BLOCK == 0 and s % (2 * Q_BLOCK) == 0 and s // Q_BLOCK <= 256
    seq = lambda w: pl.BlockSpec((s, w), lambda b: (b, 0), pipeline_mode=pl.Buffered(1))
    qa = 2 * Q_BLOCK
    return pl.pallas_call(
        functools.partial(_dsa_attn_body, topk, s),
        grid=(batch,),
        in_specs=[seq(IDX_HEADS * LANES), seq(LANES), seq(C_HEADS * C_KV_LORA), seq(LANES), seq(C_KV_LORA),
                  _full(w_uv.shape)],
        out_specs=pl.BlockSpec((s, C_HEADS * C_V), lambda b: (b, 0)),
        out_shape=jax.ShapeDtypeStruct((t, C_HEADS * C_V), BF16),
        scratch_shapes=[pltpu.VMEM((s, s), F32),
                        pltpu.VMEM((s, LANES), jnp.int32),
                        pltpu.VMEM((s, LANES), F32),
                        pltpu.VMEM((s, LANES), jnp.int32),
                        pltpu.VMEM((C_HEADS * qa, LANES), F32),
                        pltpu.VMEM((C_HEADS * qa, LANES), F32),
                        pltpu.VMEM((C_HEADS * qa, C_KV_LORA), F32)],
        compiler_params=pltpu.CompilerParams(dimension_semantics=("arbitrary",), vmem_limit_bytes=DSA_VMEM_LIMIT),
        name="dsa_attn",
    )(qi, wi, qlat, ki, ckv, w_uv)


def _mem_kv_body(mem_ref, g_ref, wk_ref, wv_ref, k_ref, v_ref):
    mn = _rms(mem_ref[...], g_ref[...]).astype(BF16)
    k_ref[...] = _dot(mn, wk_ref[...]).astype(BF16)
    v_ref[...] = _dot(mn, wv_ref[...]).astype(BF16)


def _mem_kv(mem, g, wk, wv, layer, n_mem):
    rows, d = mem.shape
    w = wk.shape[-1]
    wspec = pl.BlockSpec((None, d, w), lambda b: (layer, 0, 0))
    return pl.pallas_call(
        _mem_kv_body,
        grid=(rows // n_mem,),
        in_specs=[pl.BlockSpec((n_mem, d), lambda b: (b, 0)), _full((1, d)), wspec, wspec],
        out_specs=[pl.BlockSpec((n_mem, w), lambda b: (b, 0))] * 2,
        out_shape=[jax.ShapeDtypeStruct((rows, w), BF16)] * 2,
        compiler_params=_params("parallel"),
        name="mem_kv",
    )(mem, g, wk, wv)


def _xattn_body(x_ref, gpre_ref, gpost_ref, wq_ref, wo_ref, k_ref, v_ref, o_ref, att_scr):
    x = x_ref[...]
    h = _rms(x, gpre_ref[...]).astype(BF16)
    q = (_dot(h, wq_ref[...]) * X_HEAD_DIM ** -0.5).astype(BF16)
    for hd in range(X_HEADS):
        sl = slice(hd * X_HEAD_DIM, (hd + 1) * X_HEAD_DIM)
        s = _dot_nt(q[:, sl], k_ref[:, sl])
        p = jnp.exp(s - jnp.max(s, axis=-1, keepdims=True))
        o = _dot(p.astype(BF16), v_ref[:, sl]) / jnp.sum(p, axis=-1, keepdims=True)
        att_scr[:, sl] = o.astype(BF16)
    y = _dot(att_scr[...], wo_ref[...])
    o_ref[...] = x + _rms(y, gpost_ref[...])


def _xattn(x, gpre, gpost, wq, wo, k, v, layer, batch, *, tm):
    t, d = x.shape
    w = wq.shape[-1]
    n_mem = k.shape[0] // batch
    tiles_per_batch = t // batch // tm
    return pl.pallas_call(
        _xattn_body,
        grid=(t // tm,),
        in_specs=[pl.BlockSpec((tm, d), lambda i: (i, 0)), _full((1, d)), _full((1, d)),
                  pl.BlockSpec((None, d, w), lambda i: (layer, 0, 0)),
                  pl.BlockSpec((None, w, d), lambda i: (layer, 0, 0)),
                  pl.BlockSpec((n_mem, w), lambda i: (i // tiles_per_batch, 0)),
                  pl.BlockSpec((n_mem, w), lambda i: (i // tiles_per_batch, 0))],
        out_specs=pl.BlockSpec((tm, d), lambda i: (i, 0)),
        out_shape=jax.ShapeDtypeStruct((t, d), F32),
        scratch_shapes=[pltpu.VMEM((tm, w), BF16)],
        compiler_params=_params("parallel"),
        name="xattn",
    )(x, gpre, gpost, wq, wo, k, v)


def _pad_cols(w, width):
    return jnp.pad(w, ((0, 0), (0, width - w.shape[1])))


def _prep_ab_weights(w_in, w_uq, w_ukv, w_gates, b_gates):
    o = A_Q_LORA + A_KV_LORA
    w_in_p = jnp.concatenate([w_in[:, :o], _pad_cols(w_in[:, o:o + A_ROPE], LANES), w_in[:, o + A_ROPE:]], axis=1)
    uq = w_uq.reshape(A_Q_LORA, A_HEADS, A_NOPE + A_ROPE)
    uq = jnp.pad(uq, ((0, 0), (0, 0), (0, A_HEAD_PAD - A_NOPE - A_ROPE))).reshape(A_Q_LORA, A_HEADS * A_HEAD_PAD)
    ukv = w_ukv.reshape(A_KV_LORA, A_HEADS, A_NOPE + A_V)
    ukv = jnp.concatenate([ukv[:, :, :A_NOPE].reshape(A_KV_LORA, -1), ukv[:, :, A_NOPE:].reshape(A_KV_LORA, -1)], axis=1)
    return (w_in_p.astype(BF16), uq.astype(BF16), ukv.astype(BF16),
            _pad_cols(w_gates, LANES).astype(BF16), _pad_cols(b_gates[None, :], LANES))


def _prep_c_weights(w_in, w_uk):
    o = C_Q_LORA + C_KV_LORA
    qi = w_in[:, o:o + IDX_HEADS * IDX_DIM].reshape(-1, IDX_HEADS, IDX_DIM)
    qi = jnp.pad(qi, ((0, 0), (0, 0), (0, LANES - IDX_DIM))).reshape(-1, IDX_HEADS * LANES)
    o2 = o + IDX_HEADS * IDX_DIM
    w_in_p = jnp.concatenate([w_in[:, :o], qi, _pad_cols(w_in[:, o2:o2 + IDX_DIM], LANES),
                              _pad_cols(w_in[:, o2 + IDX_DIM:], LANES)], axis=1)
    return w_in_p.astype(BF16), jnp.swapaxes(w_uk, 1, 2).astype(BF16)


def kernel(x, mem, positions, norm_g, mem_norm_g, ffn_w_gate, ffn_w_up, ffn_w_down, xa_wq, xa_wk, xa_wv, xa_wo, ab_w_in, ab_w_out, mla_cq_g, mla_ckv_g, mla_w_uq, mla_w_ukv, ml_conv_w, ml_conv_b, ml_wq, ml_wk, ml_w_gates, ml_b_gates, ml_gn_g, ml_skip, c_w_in, c_w_out, c_cq_g, c_ckv_g, c_w_uq, c_w_uk, c_w_uv):
    batch, seq, d = x.shape
    depth = norm_g.shape[0]
    n_mem = mem.shape[1]
    t = batch * seq
    tm = min(512, seq)
    ffn_dim = ffn_w_gate.shape[-1]
    tf = ffn_dim // 2 if (ffn_dim // 2) % LANES == 0 else ffn_dim

    xt = x.reshape(t, d)
    memt = mem.reshape(batch * n_mem, d)
    pos = positions.reshape(t, 1).astype(F32)
    wg, wu, wd = ffn_w_gate.astype(BF16), ffn_w_up.astype(BF16), ffn_w_down.astype(BF16)
    xa_wq_b, xa_wk_b, xa_wv_b, xa_wo_b = (w.astype(BF16) for w in (xa_wq, xa_wk, xa_wv, xa_wo))

    for l in range(depth):
        g = norm_g[l][:, None, :]
        xt = _ffn(xt, g[0], g[1], wg, wu, wd, l, 0, tm=tm, tf=tf)
        if l % 2 == 0:
            e = l // 2
            w_in_p, uq, ukv, wgt, bgt = _prep_ab_weights(ab_w_in[e], mla_w_uq[e], mla_w_ukv[e],
                                                         ml_w_gates[e], ml_b_gates[e])
            q, k, v, xm, vm, og = _ab_prep(xt, pos, g[2], w_in_p, mla_cq_g[e][None], mla_ckv_g[e][None], uq, ukv, tm=tm)
            ya = _mla_attn(q, k, v, batch, tq=min(256, seq))
            yb = _mlstm(xm, vm, og, ml_conv_w[e], ml_conv_b[e][None], ml_wq[e].astype(BF16), ml_wk[e].astype(BF16),
                        wgt, bgt, ml_gn_g[e][None], ml_skip[e][None], batch)
            w_out = ab_w_out[e].astype(BF16)
            xt = _proj(xt, g[3], [ya, yb], [w_out[:A_HEADS * A_V], w_out[A_HEADS * A_V:]], tm=tm)
        else:
            o = l // 2
            w_in_p, w_ukt = _prep_c_weights(c_w_in[o], c_w_uk[o])
            qlat, ckv, qi, ki, wi = _dsa_prep(xt, g[2], w_in_p, c_cq_g[o][None], c_ckv_g[o][None],
                                              c_w_uq[o].astype(BF16), w_ukt, tm=tm)
            yc = _dsa_attn(qi, wi, qlat, ki, ckv, c_w_uv[o].astype(BF16), batch)
            xt = _proj(xt, g[3], [yc], [c_w_out[o].astype(BF16)], tm=tm)
        mk, mv = _mem_kv(memt, mem_norm_g[l][None], xa_wk_b, xa_wv_b, l, n_mem)
        xt = _xattn(xt, g[4], g[5], xa_wq_b, xa_wo_b, mk, mv, l, batch, tm=tm)
        xt = _ffn(xt, g[6], g[7], wg, wu, wd, l, 1, tm=tm, tf=tf)
    return xt.reshape(batch, seq, d)
```

```python
import functools
import math

import jax
import jax.numpy as jnp
from jax import lax
from jax.experimental import pallas as pl
from jax.experimental.pallas import tpu as pltpu

F32 = jnp.float32
BF16 = jnp.bfloat16

EPS = 1e-6
ROPE_BASE = 10000.0
LANES = 128
VMEM_LIMIT = 48 * 1024 * 1024

A_HEADS, A_Q_LORA, A_KV_LORA, A_NOPE, A_ROPE, A_V = 4, 384, 256, 128, 64, 128
B_HEADS, B_HEAD_DIM, B_CONV, B_CHUNK = 4, 128, 4, 128
B_WIDTH = B_HEADS * B_HEAD_DIM
C_HEADS, C_Q_LORA, C_KV_LORA, C_NOPE, C_V = 8, 384, 256, 128, 128
IDX_HEADS, IDX_DIM, TOPK_MAX = 8, 64, 256
X_HEADS, X_HEAD_DIM = 4, 128
Q_BLOCK = 128

INT_MIN = -(2 ** 31)
KEY_LOWEST_FINITE = INT_MIN + 0x00800000


def _params(*sem):
    return pltpu.CompilerParams(dimension_semantics=sem, vmem_limit_bytes=VMEM_LIMIT)


def _rms(x, g):
    return x * lax.rsqrt(jnp.mean(x * x, axis=-1, keepdims=True) + EPS) * g


def _dot(a, b):
    return jnp.dot(a, b, preferred_element_type=F32)


def _dot_nt(a, b):
    return lax.dot_general(a, b, (((1,), (1,)), ((), ())), preferred_element_type=F32)


def _dot_tn(a, b):
    return lax.dot_general(a, b, (((0,), (0,)), ((), ())), preferred_element_type=F32)


def _full(shape):
    return pl.BlockSpec(shape, lambda *_: (0,) * len(shape))


def _ffn_body(tf, x_ref, gpre_ref, gpost_ref, wg_ref, wu_ref, wd_ref, o_ref):
    x = x_ref[...]
    h = _rms(x, gpre_ref[...]).astype(BF16)
    y = None
    for j in range(wg_ref.shape[1] // tf):
        cols = slice(j * tf, (j + 1) * tf)
        g = _dot(h, wg_ref[:, cols])
        u = _dot(h, wu_ref[:, cols])
        a = (g * jax.nn.sigmoid(g) * u).astype(BF16)
        part = _dot(a, wd_ref[cols, :])
        y = part if y is None else y + part
    o_ref[...] = x + 0.5 * _rms(y, gpost_ref[...])


def _ffn(x, gpre, gpost, wg, wu, wd, layer, slot, *, tm, tf):
    t, d = x.shape
    f = wg.shape[-1]
    resident = lambda r, c: pl.BlockSpec((None, None, r, c), lambda i: (layer, slot, 0, 0),
                                         pipeline_mode=pl.Buffered(1))
    return pl.pallas_call(
        functools.partial(_ffn_body, tf),
        grid=(t // tm,),
        in_specs=[pl.BlockSpec((tm, d), lambda i: (i, 0)), _full((1, d)), _full((1, d)),
                  resident(d, f), resident(d, f), resident(f, d)],
        out_specs=pl.BlockSpec((tm, d), lambda i: (i, 0)),
        out_shape=jax.ShapeDtypeStruct((t, d), F32),
        compiler_params=_params("parallel"),
        name="ffn",
    )(x, gpre, gpost, wg, wu, wd)


def _proj_body(n_in, x_ref, g_ref, *refs):
    a_refs, w_refs, o_ref = refs[:n_in], refs[n_in:2 * n_in], refs[2 * n_in]
    y = _dot(a_refs[0][...], w_refs[0][...])
    for a_ref, w_ref in zip(a_refs[1:], w_refs[1:]):
        y += _dot(a_ref[...], w_ref[...])
    o_ref[...] = x_ref[...] + _rms(y, g_ref[...])


def _proj(x, g, acts, weights, *, tm):
    t, d = x.shape
    n_in = len(acts)
    return pl.pallas_call(
        functools.partial(_proj_body, n_in),
        grid=(t // tm,),
        in_specs=[pl.BlockSpec((tm, d), lambda i: (i, 0)), _full((1, d))]
        + [pl.BlockSpec((tm, a.shape[1]), lambda i: (i, 0)) for a in acts]
        + [_full(w.shape) for w in weights],
        out_specs=pl.BlockSpec((tm, d), lambda i: (i, 0)),
        out_shape=jax.ShapeDtypeStruct((t, d), F32),
        compiler_params=_params("parallel"),
        name="mixer_out_proj",
    )(x, g, *acts, *weights)


def _rope_tables(pos):
    half = A_ROPE // 2
    lane = lax.broadcasted_iota(jnp.int32, (1, LANES), 1)
    j = (lane % half).astype(F32)
    inv = jnp.exp(j * (-math.log(ROPE_BASE) / half))
    ang = pos * inv
    sin = jnp.where(lane < A_ROPE, jnp.sin(ang), 0.0)
    return jnp.cos(ang), sin


def _rope_apply(r, cos, sin):
    half = A_ROPE // 2
    return r * cos + (pltpu.roll(r, half, axis=1) - pltpu.roll(r, LANES - half, axis=1)) * sin


AB_Z = A_Q_LORA + A_KV_LORA + LANES + 3 * B_WIDTH
A_HEAD_PAD = 2 * LANES


def _ab_prep_body(x_ref, pos_ref, g_ref, w_in_ref, cq_g_ref, ckv_g_ref, w_uq_ref, w_ukv_ref,
                  q_ref, k_ref, v_ref, xm_ref, vm_ref, og_ref):
    h = _rms(x_ref[...], g_ref[...]).astype(BF16)
    z = _dot(h, w_in_ref[...])
    o = 0
    cq = z[:, o:o + A_Q_LORA]; o += A_Q_LORA
    ckv = z[:, o:o + A_KV_LORA]; o += A_KV_LORA
    kr = z[:, o:o + LANES]; o += LANES
    xm_ref[...] = z[:, o:o + B_WIDTH]; o += B_WIDTH
    vm_ref[...] = z[:, o:o + B_WIDTH]; o += B_WIDTH
    og_ref[...] = z[:, o:o + B_WIDTH]

    cos, sin = _rope_tables(pos_ref[...])
    scale = (A_NOPE + A_ROPE) ** -0.5
    q = _dot(_rms(cq, cq_g_ref[...]).astype(BF16), w_uq_ref[...]) * scale
    kv = _dot(_rms(ckv, ckv_g_ref[...]).astype(BF16), w_ukv_ref[...])
    k_rope = _rope_apply(kr, cos, sin).astype(BF16)
    for hd in range(A_HEADS):
        base = hd * A_HEAD_PAD
        q_ref[:, base:base + A_NOPE] = q[:, base:base + A_NOPE].astype(BF16)
        q_ref[:, base + A_NOPE:base + A_HEAD_PAD] = _rope_apply(
            q[:, base + A_NOPE:base + A_HEAD_PAD], cos, sin).astype(BF16)
        k_ref[:, base:base + A_NOPE] = kv[:, hd * A_NOPE:(hd + 1) * A_NOPE].astype(BF16)
        k_ref[:, base + A_NOPE:base + A_HEAD_PAD] = k_rope
    v_ref[...] = kv[:, A_HEADS * A_NOPE:].astype(BF16)


def _ab_prep(x, pos, g, w_in, cq_g, ckv_g, w_uq, w_ukv, *, tm):
    t, d = x.shape
    row = lambda w: pl.BlockSpec((tm, w), lambda i: (i, 0))
    return pl.pallas_call(
        _ab_prep_body,
        grid=(t // tm,),
        in_specs=[row(d), row(1), _full((1, d)), _full(w_in.shape), _full(cq_g.shape), _full(ckv_g.shape),
                  _full(w_uq.shape), _full(w_ukv.shape)],
        out_specs=[row(A_HEADS * A_HEAD_PAD), row(A_HEADS * A_HEAD_PAD), row(A_HEADS * A_V),
                   row(B_WIDTH), row(B_WIDTH), row(B_WIDTH)],
        out_shape=[jax.ShapeDtypeStruct((t, A_HEADS * A_HEAD_PAD), BF16),
                   jax.ShapeDtypeStruct((t, A_HEADS * A_HEAD_PAD), BF16),
                   jax.ShapeDtypeStruct((t, A_HEADS * A_V), BF16),
                   jax.ShapeDtypeStruct((t, B_WIDTH), F32),
                   jax.ShapeDtypeStruct((t, B_WIDTH), F32),
                   jax.ShapeDtypeStruct((t, B_WIDTH), F32)],
        compiler_params=_params("parallel"),
        name="ab_prep",
    )(x, pos, g, w_in, cq_g, ckv_g, w_uq, w_ukv)


def _mla_attn_body(tq, q_ref, k_ref, v_ref, o_ref, acc_scr):
    i = pl.program_id(1)
    heads = range(A_HEADS)
    acc_scr[...] = jnp.zeros(acc_scr.shape, F32)

    def chunk(j, carry, masked):
        ms, ls = carry
        start = pl.multiple_of(j * tq, tq)
        new_ms, new_ls = [], []
        for h in heads:
            s = _dot_nt(q_ref[:, h * A_HEAD_PAD:(h + 1) * A_HEAD_PAD],
                        k_ref[pl.ds(start, tq), h * A_HEAD_PAD:(h + 1) * A_HEAD_PAD])
            if masked:
                row = lax.broadcasted_iota(jnp.int32, (tq, tq), 0)
                col = lax.broadcasted_iota(jnp.int32, (tq, tq), 1)
                s = jnp.where(col <= row, s, -jnp.inf)
            m_new = jnp.maximum(ms[h], jnp.max(s, axis=-1, keepdims=True))
            alpha = jnp.exp(ms[h] - m_new)
            p = jnp.exp(s - m_new)
            new_ls.append(alpha * ls[h] + jnp.sum(p, axis=-1, keepdims=True))
            acc_scr[h] = alpha * acc_scr[h] + _dot(p.astype(BF16), v_ref[pl.ds(start, tq), h * A_V:(h + 1) * A_V])
            new_ms.append(m_new)
        return tuple(new_ms), tuple(new_ls)

    init = (tuple(jnp.full((tq, 1), -jnp.inf, F32) for _ in heads), tuple(jnp.zeros((tq, 1), F32) for _ in heads))
    carry = lax.fori_loop(0, i, lambda j, c: chunk(j, c, False), init)
    _, ls = chunk(i, carry, True)
    for h in heads:
        o_ref[:, h * A_V:(h + 1) * A_V] = (acc_scr[h] / ls[h]).astype(BF16)


def _mla_attn(q, k, v, batch, *, tq):
    t = q.shape[0]
    s = t // batch
    nq = s // tq
    return pl.pallas_call(
        functools.partial(_mla_attn_body, tq),
        grid=(batch, nq),
        in_specs=[pl.BlockSpec((tq, A_HEADS * A_HEAD_PAD), lambda b, i: (b * nq + i, 0)),
                  pl.BlockSpec((s, A_HEADS * A_HEAD_PAD), lambda b, i: (b, 0)),
                  pl.BlockSpec((s, A_HEADS * A_V), lambda b, i: (b, 0))],
        out_specs=pl.BlockSpec((tq, A_HEADS * A_V), lambda b, i: (b * nq + i, 0)),
        out_shape=jax.ShapeDtypeStruct((t, A_HEADS * A_V), BF16),
        scratch_shapes=[pltpu.VMEM((A_HEADS, tq, A_V), F32)],
        compiler_params=_params("parallel", "arbitrary"),
        name="mla_attn",
    )(q, k, v)


def _log_sigmoid(x):
    return jnp.minimum(x, 0.0) - jnp.log(1.0 + jnp.exp(-jnp.abs(x)))


def _mlstm_body(xm_ref, vm_ref, og_ref, conv_w_ref, conv_b_ref, wq_ref, wk_ref, wgt_ref, bgt_ref,
                gn_ref, skip_ref, y_ref, xext_scr, gin_scr, state_scr, m_scr):
    c = pl.program_id(1)
    L = B_CHUNK
    tail = 8

    @pl.when(c == 0)
    def _():
        xext_scr[0:tail, :] = jnp.zeros((tail, B_WIDTH), F32)
        state_scr[...] = jnp.zeros_like(state_scr)
        m_scr[...] = jnp.zeros_like(m_scr)

    xm = xm_ref[...]
    xext_scr[tail:tail + L, :] = xm
    conv = conv_b_ref[...] + conv_w_ref[B_CONV - 1:B_CONV, :] * xm
    for tap in range(B_CONV - 1):
        shift = B_CONV - 1 - tap
        conv += conv_w_ref[tap:tap + 1, :] * xext_scr[tail - shift:tail - shift + L, :]
    xext_scr[0:tail, :] = xm[L - tail:, :]
    xc = conv * jax.nn.sigmoid(conv)
    xcb = xc.astype(BF16)

    vm = vm_ref[...]
    qs, ks = [], []
    for h in range(B_HEADS):
        sl = slice(h * B_HEAD_DIM, (h + 1) * B_HEAD_DIM)
        qh = _dot(xcb[:, sl], wq_ref[h])
        kh = _dot(xcb[:, sl], wk_ref[h])
        qs.append(qh)
        ks.append(kh)
        gin_scr[:, h * B_HEAD_DIM:(h + 1) * B_HEAD_DIM] = qh.astype(BF16)
        gin_scr[:, B_WIDTH + h * B_HEAD_DIM:B_WIDTH + (h + 1) * B_HEAD_DIM] = kh.astype(BF16)
    gin_scr[:, 2 * B_WIDTH:] = vm.astype(BF16)
    gates = _dot(gin_scr[...], wgt_ref[...]) + bgt_ref[...]

    lane = lax.broadcasted_iota(jnp.int32, (L, LANES), 1)
    row = lax.broadcasted_iota(jnp.int32, (L, LANES), 0)
    is_f = (lane >= B_HEADS) & (lane < 2 * B_HEADS)
    glog = jnp.where(is_f, _log_sigmoid(gates), gates)
    bcum = jnp.where(is_f, glog, 0.0)
    step = 1
    while step < L:
        bcum = bcum + jnp.where(row >= step, pltpu.roll(bcum, step, axis=0), 0.0)
        step *= 2
    glog_t = glog.T
    bcum_t = bcum.T

    causal = lax.broadcasted_iota(jnp.int32, (L, L), 1) <= lax.broadcasted_iota(jnp.int32, (L, L), 0)
    ones_col = jnp.where(lax.broadcasted_iota(jnp.int32, (L, LANES), 1) == 0, 1.0, 0.0).astype(BF16)
    kscale = B_HEAD_DIM ** -0.5
    for h in range(B_HEADS):
        sl = slice(h * B_HEAD_DIM, (h + 1) * B_HEAD_DIM)
        qb = qs[h].astype(BF16)
        k_s = ks[h] * kscale
        v_ext = jnp.concatenate([vm[:, sl].astype(BF16), ones_col], axis=1)
        li_c = glog[:, h:h + 1]
        b_c = bcum[:, B_HEADS + h:B_HEADS + h + 1]
        li_r = glog_t[h:h + 1, :]
        b_r = bcum_t[B_HEADS + h:B_HEADS + h + 1, :]
        b_tot = b_r[:, L - 1:L]
        m_st = m_scr[h:h + 1, 0:1]
        state = state_scr[h]

        dmat = jnp.where(causal, b_c - b_r + li_r, -jnp.inf)
        m_inter = b_c + m_st
        m_j = jnp.maximum(m_inter, jnp.max(dmat, axis=-1, keepdims=True))
        sc = _dot_nt(qb, k_s.astype(BF16)) * jnp.exp(dmat - m_j)
        inter = jnp.exp(m_inter - m_j)
        ne = inter * _dot(qb, state.astype(BF16)) + _dot(sc.astype(BF16), v_ext)
        den = ne[:, B_HEAD_DIM:B_HEAD_DIM + 1]
        hj = ne[:, :B_HEAD_DIM] / jnp.maximum(jnp.abs(den), jnp.exp(-m_j))

        g_c = b_tot - b_c + li_c
        g_r = b_tot - b_r + li_r
        m_new = jnp.maximum(b_tot + m_st, jnp.max(g_r, axis=-1, keepdims=True))
        decay = jnp.exp(b_tot + m_st - m_new)
        kw = (k_s * jnp.exp(g_c - m_new)).astype(BF16)
        state_scr[h] = decay * state + _dot_tn(kw, v_ext)
        m_scr[h:h + 1, :] = jnp.broadcast_to(m_new, (1, LANES))

        mu = jnp.mean(hj, axis=-1, keepdims=True)
        var = jnp.mean(jnp.square(hj - mu), axis=-1, keepdims=True)
        hn = (hj - mu) * lax.rsqrt(var + EPS)
        out = (hn * gn_ref[:, sl] + skip_ref[:, sl] * xc[:, sl]) * jax.nn.sigmoid(og_ref[:, sl])
        y_ref[:, sl] = out.astype(BF16)


def _mlstm(xm, vm, og, conv_w, conv_b, wq, wk, wgt, bgt, gn_g, skip, batch):
    t = xm.shape[0]
    nc = t // batch // B_CHUNK
    row = pl.BlockSpec((B_CHUNK, B_WIDTH), lambda b, c: (b * nc + c, 0))
    return pl.pallas_call(
        _mlstm_body,
        grid=(batch, nc),
        in_specs=[row, row, row, _full(conv_w.shape), _full(conv_b.shape), _full(wq.shape), _full(wk.shape),
                  _full(wgt.shape), _full(bgt.shape), _full(gn_g.shape), _full(skip.shape)],
        out_specs=row,
        out_shape=jax.ShapeDtypeStruct((t, B_WIDTH), BF16),
        scratch_shapes=[pltpu.VMEM((8 + B_CHUNK, B_WIDTH), F32),
                        pltpu.VMEM((B_CHUNK, 3 * B_WIDTH), BF16),
                        pltpu.VMEM((B_HEADS, B_HEAD_DIM, 2 * LANES), F32),
                        pltpu.VMEM((8, LANES), F32)],
        compiler_params=_params("parallel", "arbitrary"),
        name="mlstm",
    )(xm, vm, og, conv_w, conv_b, wq, wk, wgt, bgt, gn_g, skip)


C_Z = C_Q_LORA + C_KV_LORA + IDX_HEADS * LANES + LANES + LANES


def _dsa_prep_body(x_ref, g_ref, w_in_ref, cq_g_ref, ckv_g_ref, w_uq_ref, w_ukt_ref,
                   qlat_ref, ckv_ref, qi_ref, ki_ref, wi_ref):
    h = _rms(x_ref[...], g_ref[...]).astype(BF16)
    z = _dot(h, w_in_ref[...])
    o = 0
    cq = z[:, o:o + C_Q_LORA]; o += C_Q_LORA
    ckv = z[:, o:o + C_KV_LORA]; o += C_KV_LORA
    qi_ref[...] = z[:, o:o + IDX_HEADS * LANES].astype(BF16); o += IDX_HEADS * LANES
    ki_ref[...] = z[:, o:o + LANES].astype(BF16); o += LANES
    wi_ref[...] = z[:, o:o + LANES]
    ckv_ref[...] = _rms(ckv, ckv_g_ref[...]).astype(BF16)
    q = _dot(_rms(cq, cq_g_ref[...]).astype(BF16), w_uq_ref[...]).astype(BF16)
    scale = C_NOPE ** -0.5
    for hd in range(C_HEADS):
        ql = _dot(q[:, hd * C_NOPE:(hd + 1) * C_NOPE], w_ukt_ref[hd]) * scale
        qlat_ref[:, hd * C_KV_LORA:(hd + 1) * C_KV_LORA] = ql.astype(BF16)


def _dsa_prep(x, g, w_in, cq_g, ckv_g, w_uq, w_ukt, *, tm):
    t, d = x.shape
    row = lambda w: pl.BlockSpec((tm, w), lambda i: (i, 0))
    return pl.pallas_call(
        _dsa_prep_body,
        grid=(t // tm,),
        in_specs=[row(d), _full((1, d)), _full(w_in.shape), _full(cq_g.shape), _full(ckv_g.shape),
                  _full(w_uq.shape), _full(w_ukt.shape)],
        out_specs=[row(C_HEADS * C_KV_LORA), row(C_KV_LORA), row(IDX_HEADS * LANES), row(LANES), row(LANES)],
        out_shape=[jax.ShapeDtypeStruct((t, C_HEADS * C_KV_LORA), BF16),
                   jax.ShapeDtypeStruct((t, C_KV_LORA), BF16),
                   jax.ShapeDtypeStruct((t, IDX_HEADS * LANES), BF16),
                   jax.ShapeDtypeStruct((t, LANES), BF16),
                   jax.ShapeDtypeStruct((t, LANES), F32)],
        compiler_params=_params("parallel"),
        name="dsa_prep",
    )(x, g, w_in, cq_g, ckv_g, w_uq, w_ukt)


INT_MAX = 2 ** 31 - 1
DSA_VMEM_LIMIT = 56 * 1024 * 1024


def _key_to_f32(key):
    return pltpu.bitcast(key ^ ((key >> 31) & 0x7FFFFFFF), F32)


def _dsa_attn_body(topk, seq, qi_ref, wi_ref, qlat_ref, ki_ref, ckv_ref, w_uv_ref, o_ref,
                   score_scr, key_scr, cnt_scr, bound_scr, m_scr, l_scr, acc_scr):
    Q = Q_BLOCK
    H = C_HEADS
    nb = seq // Q
    first = topk // Q
    col_iota = lax.broadcasted_iota(jnp.int32, (Q, Q), 1)
    ones = jnp.ones((Q, LANES), BF16)
    topk_f = float(topk)

    qa = 2 * Q
    qa_col = lax.broadcasted_iota(jnp.int32, (qa, qa), 1)
    qa_row = lax.broadcasted_iota(jnp.int32, (qa, qa), 0)

    def score_block(blk, _):
        r0 = pl.multiple_of(blk * qa, qa)
        qi = jnp.concatenate([qi_ref[pl.ds(r0, qa), h * LANES:(h + 1) * LANES] for h in range(IDX_HEADS)], axis=0)
        wi = wi_ref[pl.ds(r0, qa), :]
        wb = [jnp.broadcast_to(wi[:, h:h + 1], (qa, qa)) for h in range(IDX_HEADS)]

        def tile(c, _):
            k0 = pl.multiple_of(c * qa, qa)
            rel = jnp.maximum(_dot_nt(qi, ki_ref[pl.ds(k0, qa), :]), 0.0)
            score = rel[0:qa] * wb[0]
            for h in range(1, IDX_HEADS):
                score += rel[h * qa:(h + 1) * qa] * wb[h]
            score = jnp.where(k0 + qa_col <= r0 + qa_row, score, -jnp.inf)
            score_scr[pl.ds(r0, qa), pl.ds(k0, qa)] = score
            return 0

        lax.fori_loop(0, blk + 1, tile, 0)
        return 0

    lax.fori_loop(0, seq // qa, score_block, 0)

    if first > 0:
        key_scr[0:first * Q, :] = jnp.full((first * Q, LANES), KEY_LOWEST_FINITE, jnp.int32)
        cnt_scr[0:first * Q, :] = jnp.full((first * Q, LANES), topk_f, F32)
    for i in range(first, nb):
        key_scr[i * Q:(i + 1) * Q, :] = jnp.full((Q, LANES), INT_MIN, jnp.int32)
        cnt_scr[i * Q:(i + 1) * Q, :] = jnp.full((Q, LANES), float((i + 1) * Q), F32)

    def bit_step(b, _):
        bit = jnp.left_shift(jnp.int32(1), 31 - b)
        for i in range(first, nb):
            rows = slice(i * Q, (i + 1) * Q)
            key = key_scr[rows, :]
            cand = jnp.where(b == 0, jnp.zeros_like(key), key | bit)
            cand_f = _key_to_f32(cand)
            hits = jnp.zeros((Q, LANES), F32)
            for c in range(i + 1):
                hits += jnp.where(score_scr[rows, c * Q:(c + 1) * Q] >= cand_f, 1.0, 0.0)
            cnt = _dot(hits.astype(BF16), ones)
            ok = cnt >= topk_f
            key_scr[rows, :] = jnp.where(ok, cand, key)
            cnt_scr[rows, :] = jnp.where(ok, cnt, cnt_scr[rows, :])
        return 0

    lax.fori_loop(0, 32, bit_step, 0)

    bound_scr[...] = jnp.full(bound_scr.shape, INT_MAX, jnp.int32)

    idx_bits = (seq - 1).bit_length()

    def tie_block(i, _):
        r0 = pl.multiple_of(i * Q, Q)

        @pl.when(jnp.max(cnt_scr[pl.ds(r0, Q), :]) > topk_f)
        def _():
            thr = _key_to_f32(key_scr[pl.ds(r0, Q), :])

            def count(pred_fn):
                def body(c, hits):
                    k0 = pl.multiple_of(c * Q, Q)
                    return hits + jnp.where(pred_fn(score_scr[pl.ds(r0, Q), pl.ds(k0, Q)], k0), 1.0, 0.0)
                return _dot(lax.fori_loop(0, i + 1, body, jnp.zeros((Q, LANES), F32)).astype(BF16), ones)

            need = topk_f - count(lambda sc, k0: sc > thr)

            def pos_step(b, bound):
                cand = bound - jnp.right_shift(jnp.int32(1 << (idx_bits - 1)), b)
                cnt = count(lambda sc, k0: (sc == thr) & (k0 + col_iota <= cand))
                return jnp.where(cnt >= need, cand, bound)

            all_keys = jnp.full((Q, LANES), (1 << idx_bits) - 1, jnp.int32)
            bound_scr[pl.ds(r0, Q), :] = lax.fori_loop(0, idx_bits, pos_step, all_keys)
        return 0

    lax.fori_loop(first, nb, tie_block, 0)

    twice = lambda a: jnp.concatenate([a, a], axis=1)

    def attn_block(blk, _):
        r0 = pl.multiple_of(blk * qa, qa)
        qlat = jnp.concatenate([qlat_ref[pl.ds(r0, qa), h * C_KV_LORA:(h + 1) * C_KV_LORA] for h in range(H)], axis=0)
        thr = twice(_key_to_f32(key_scr[pl.ds(r0, qa), :]))
        bound = twice(bound_scr[pl.ds(r0, qa), :])
        m_scr[...] = jnp.full(m_scr.shape, -jnp.inf, F32)
        l_scr[...] = jnp.zeros(l_scr.shape, F32)
        acc_scr[...] = jnp.zeros(acc_scr.shape, F32)

        def tile(c, _):
            k0 = pl.multiple_of(c * qa, qa)
            sc = score_scr[pl.ds(r0, qa), pl.ds(k0, qa)]
            sel = (sc > thr) | ((sc == thr) & (k0 + qa_col <= bound))
            ckv = ckv_ref[pl.ds(k0, qa), :]
            logits = _dot_nt(qlat, ckv)
            for h in range(H):
                rs = slice(h * qa, (h + 1) * qa)
                s = jnp.where(sel, logits[rs], -jnp.inf)
                m_old = m_scr[rs]
                m_new = jnp.maximum(m_old, jnp.max(s, axis=-1, keepdims=True))
                m_safe = jnp.where(m_new == -jnp.inf, 0.0, m_new)
                alpha = jnp.exp(m_old - m_safe)
                p = jnp.exp(s - twice(m_safe))
                l_scr[rs] = alpha * l_scr[rs] + jnp.sum(p, axis=-1, keepdims=True)
                acc_scr[rs] = twice(alpha) * acc_scr[rs] + _dot(p.astype(BF16), ckv)
                m_scr[rs] = m_new
            return 0

        lax.fori_loop(0, blk + 1, tile, 0)

        for h in range(H):
            rs = slice(h * qa, (h + 1) * qa)
            o_lat = (acc_scr[rs] / twice(l_scr[rs])).astype(BF16)
            o_ref[pl.ds(r0, qa), h * C_V:(h + 1) * C_V] = _dot(o_lat, w_uv_ref[h]).astype(BF16)
        return 0

    lax.fori_loop(0, seq // qa, attn_block, 0)


def _dsa_attn(qi, wi, qlat, ki, ckv, w_uv, batch):
    t = qi.shape[0]
    s = t // batch
    topk = min(TOPK_MAX, s // 4)
    assert topk % Q_BLOCK == 0 and s % (2 * Q_BLOCK) == 0 and s // Q_BLOCK <= 256
    seq = lambda w: pl.BlockSpec((s, w), lambda b: (b, 0), pipeline_mode=pl.Buffered(1))
    qa = 2 * Q_BLOCK
    return pl.pallas_call(
        functools.partial(_dsa_attn_body, topk, s),
        grid=(batch,),
        in_specs=[seq(IDX_HEADS * LANES), seq(LANES), seq(C_HEADS * C_KV_LORA), seq(LANES), seq(C_KV_LORA),
                  _full(w_uv.shape)],
        out_specs=pl.BlockSpec((s, C_HEADS * C_V), lambda b: (b, 0)),
        out_shape=jax.ShapeDtypeStruct((t, C_HEADS * C_V), BF16),
        scratch_shapes=[pltpu.VMEM((s, s), F32),
                        pltpu.VMEM((s, LANES), jnp.int32),
                        pltpu.VMEM((s, LANES), F32),
                        pltpu.VMEM((s, LANES), jnp.int32),
                        pltpu.VMEM((C_HEADS * qa, LANES), F32),
                        pltpu.VMEM((C_HEADS * qa, LANES), F32),
                        pltpu.VMEM((C_HEADS * qa, C_KV_LORA), F32)],
        compiler_params=pltpu.CompilerParams(dimension_semantics=("arbitrary",), vmem_limit_bytes=DSA_VMEM_LIMIT),
        name="dsa_attn",
    )(qi, wi, qlat, ki, ckv, w_uv)


def _mem_kv_body(mem_ref, g_ref, wk_ref, wv_ref, k_ref, v_ref):
    mn = _rms(mem_ref[...], g_ref[...]).astype(BF16)
    k_ref[...] = _dot(mn, wk_ref[...]).astype(BF16)
    v_ref[...] = _dot(mn, wv_ref[...]).astype(BF16)


def _mem_kv(mem, g, wk, wv, layer, n_mem):
    rows, d = mem.shape
    w = wk.shape[-1]
    wspec = pl.BlockSpec((None, d, w), lambda b: (layer, 0, 0))
    return pl.pallas_call(
        _mem_kv_body,
        grid=(rows // n_mem,),
        in_specs=[pl.BlockSpec((n_mem, d), lambda b: (b, 0)), _full((1, d)), wspec, wspec],
        out_specs=[pl.BlockSpec((n_mem, w), lambda b: (b, 0))] * 2,
        out_shape=[jax.ShapeDtypeStruct((rows, w), BF16)] * 2,
        compiler_params=_params("parallel"),
        name="mem_kv",
    )(mem, g, wk, wv)


def _xattn_body(x_ref, gpre_ref, gpost_ref, wq_ref, wo_ref, k_ref, v_ref, o_ref, att_scr):
    x = x_ref[...]
    h = _rms(x, gpre_ref[...]).astype(BF16)
    q = (_dot(h, wq_ref[...]) * X_HEAD_DIM ** -0.5).astype(BF16)
    for hd in range(X_HEADS):
        sl = slice(hd * X_HEAD_DIM, (hd + 1) * X_HEAD_DIM)
        s = _dot_nt(q[:, sl], k_ref[:, sl])
        p = jnp.exp(s - jnp.max(s, axis=-1, keepdims=True))
        o = _dot(p.astype(BF16), v_ref[:, sl]) / jnp.sum(p, axis=-1, keepdims=True)
        att_scr[:, sl] = o.astype(BF16)
    y = _dot(att_scr[...], wo_ref[...])
    o_ref[...] = x + _rms(y, gpost_ref[...])


def _xattn(x, gpre, gpost, wq, wo, k, v, layer, batch, *, tm):
    t, d = x.shape
    w = wq.shape[-1]
    n_mem = k.shape[0] // batch
    tiles_per_batch = t // batch // tm
    return pl.pallas_call(
        _xattn_body,
        grid=(t // tm,),
        in_specs=[pl.BlockSpec((tm, d), lambda i: (i, 0)), _full((1, d)), _full((1, d)),
                  pl.BlockSpec((None, d, w), lambda i: (layer, 0, 0)),
                  pl.BlockSpec((None, w, d), lambda i: (layer, 0, 0)),
                  pl.BlockSpec((n_mem, w), lambda i: (i // tiles_per_batch, 0)),
                  pl.BlockSpec((n_mem, w), lambda i: (i // tiles_per_batch, 0))],
        out_specs=pl.BlockSpec((tm, d), lambda i: (i, 0)),
        out_shape=jax.ShapeDtypeStruct((t, d), F32),
        scratch_shapes=[pltpu.VMEM((tm, w), BF16)],
        compiler_params=_params("parallel"),
        name="xattn",
    )(x, gpre, gpost, wq, wo, k, v)


def _pad_cols(w, width):
    return jnp.pad(w, ((0, 0), (0, width - w.shape[1])))


def _prep_ab_weights(w_in, w_uq, w_ukv, w_gates, b_gates):
    o = A_Q_LORA + A_KV_LORA
    w_in_p = jnp.concatenate([w_in[:, :o], _pad_cols(w_in[:, o:o + A_ROPE], LANES), w_in[:, o + A_ROPE:]], axis=1)
    uq = w_uq.reshape(A_Q_LORA, A_HEADS, A_NOPE + A_ROPE)
    uq = jnp.pad(uq, ((0, 0), (0, 0), (0, A_HEAD_PAD - A_NOPE - A_ROPE))).reshape(A_Q_LORA, A_HEADS * A_HEAD_PAD)
    ukv = w_ukv.reshape(A_KV_LORA, A_HEADS, A_NOPE + A_V)
    ukv = jnp.concatenate([ukv[:, :, :A_NOPE].reshape(A_KV_LORA, -1), ukv[:, :, A_NOPE:].reshape(A_KV_LORA, -1)], axis=1)
    return (w_in_p.astype(BF16), uq.astype(BF16), ukv.astype(BF16),
            _pad_cols(w_gates, LANES).astype(BF16), _pad_cols(b_gates[None, :], LANES))


def _prep_c_weights(w_in, w_uk):
    o = C_Q_LORA + C_KV_LORA
    qi = w_in[:, o:o + IDX_HEADS * IDX_DIM].reshape(-1, IDX_HEADS, IDX_DIM)
    qi = jnp.pad(qi, ((0, 0), (0, 0), (0, LANES - IDX_DIM))).reshape(-1, IDX_HEADS * LANES)
    o2 = o + IDX_HEADS * IDX_DIM
    w_in_p = jnp.concatenate([w_in[:, :o], qi, _pad_cols(w_in[:, o2:o2 + IDX_DIM], LANES),
                              _pad_cols(w_in[:, o2 + IDX_DIM:], LANES)], axis=1)
    return w_in_p.astype(BF16), jnp.swapaxes(w_uk, 1, 2).astype(BF16)


def kernel(x, mem, positions, norm_g, mem_norm_g, ffn_w_gate, ffn_w_up, ffn_w_down, xa_wq, xa_wk, xa_wv, xa_wo, ab_w_in, ab_w_out, mla_cq_g, mla_ckv_g, mla_w_uq, mla_w_ukv, ml_conv_w, ml_conv_b, ml_wq, ml_wk, ml_w_gates, ml_b_gates, ml_gn_g, ml_skip, c_w_in, c_w_out, c_cq_g, c_ckv_g, c_w_uq, c_w_uk, c_w_uv):
    batch, seq, d = x.shape
    depth = norm_g.shape[0]
    n_mem = mem.shape[1]
    t = batch * seq
    tm = min(512, seq)
    ffn_dim = ffn_w_gate.shape[-1]
    tf = ffn_dim // 2 if (ffn_dim // 2) % LANES == 0 else ffn_dim

    xt = x.reshape(t, d)
    memt = mem.reshape(batch * n_mem, d)
    pos = positions.reshape(t, 1).astype(F32)
    wg, wu, wd = ffn_w_gate.astype(BF16), ffn_w_up.astype(BF16), ffn_w_down.astype(BF16)
    xa_wq_b, xa_wk_b, xa_wv_b, xa_wo_b = (w.astype(BF16) for w in (xa_wq, xa_wk, xa_wv, xa_wo))

    for l in range(depth):
        g = norm_g[l][:, None, :]
        xt = _ffn(xt, g[0], g[1], wg, wu, wd, l, 0, tm=tm, tf=tf)
        if l % 2 == 0:
            e = l // 2
            w_in_p, uq, ukv, wgt, bgt = _prep_ab_weights(ab_w_in[e], mla_w_uq[e], mla_w_ukv[e],
                                                         ml_w_gates[e], ml_b_gates[e])
            q, k, v, xm, vm, og = _ab_prep(xt, pos, g[2], w_in_p, mla_cq_g[e][None], mla_ckv_g[e][None], uq, ukv, tm=tm)
            ya = _mla_attn(q, k, v, batch, tq=min(256, seq))
            yb = _mlstm(xm, vm, og, ml_conv_w[e], ml_conv_b[e][None], ml_wq[e].astype(BF16), ml_wk[e].astype(BF16),
                        wgt, bgt, ml_gn_g[e][None], ml_skip[e][None], batch)
            w_out = ab_w_out[e].astype(BF16)
            xt = _proj(xt, g[3], [ya, yb], [w_out[:A_HEADS * A_V], w_out[A_HEADS * A_V:]], tm=tm)
        else:
            o = l // 2
            w_in_p, w_ukt = _prep_c_weights(c_w_in[o], c_w_uk[o])
            qlat, ckv, qi, ki, wi = _dsa_prep(xt, g[2], w_in_p, c_cq_g[o][None], c_ckv_g[o][None],
                                              c_w_uq[o].astype(BF16), w_ukt, tm=tm)
            yc = _dsa_attn(qi, wi, qlat, ki, ckv, c_w_uv[o].astype(BF16), batch)
            xt = _proj(xt, g[3], [yc], [c_w_out[o].astype(BF16)], tm=tm)
        mk, mv = _mem_kv(memt, mem_norm_g[l][None], xa_wk_b, xa_wv_b, l, n_mem)
        xt = _xattn(xt, g[4], g[5], xa_wq_b, xa_wo_b, mk, mv, l, batch, tm=tm)
        xt = _ffn(xt, g[6], g[7], wg, wu, wd, l, 1, tm=tm, tf=tf)
    return xt.reshape(batch, seq, d)
```

```python
import functools
import math

import jax
import jax.numpy as jnp
from jax import lax
from jax.experimental import pallas as pl
from jax.experimental.pallas import tpu as pltpu

F32 = jnp.float32
BF16 = jnp.bfloat16

EPS = 1e-6
ROPE_BASE = 10000.0
LANES = 128
VMEM_LIMIT = 48 * 1024 * 1024

A_HEADS, A_Q_LORA, A_KV_LORA, A_NOPE, A_ROPE, A_V = 4, 384, 256, 128, 64, 128
B_HEADS, B_HEAD_DIM, B_CONV, B_CHUNK = 4, 128, 4, 128
B_WIDTH = B_HEADS * B_HEAD_DIM
C_HEADS, C_Q_LORA, C_KV_LORA, C_NOPE, C_V = 8, 384, 256, 128, 128
IDX_HEADS, IDX_DIM, TOPK_MAX = 8, 64, 256
X_HEADS, X_HEAD_DIM = 4, 128
Q_BLOCK = 128

INT_MIN = -(2 ** 31)
KEY_LOWEST_FINITE = INT_MIN + 0x00800000


def _params(*sem):
    return pltpu.CompilerParams(dimension_semantics=sem, vmem_limit_bytes=VMEM_LIMIT)


def _rms(x, g):
    return x * lax.rsqrt(jnp.mean(x * x, axis=-1, keepdims=True) + EPS) * g


def _dot(a, b):
    return jnp.dot(a, b, preferred_element_type=F32)


def _dot_nt(a, b):
    return lax.dot_general(a, b, (((1,), (1,)), ((), ())), preferred_element_type=F32)


def _dot_tn(a, b):
    return lax.dot_general(a, b, (((0,), (0,)), ((), ())), preferred_element_type=F32)


def _full(shape):
    return pl.BlockSpec(shape, lambda *_: (0,) * len(shape))


def _ffn_body(tf, x_ref, gpre_ref, gpost_ref, wg_ref, wu_ref, wd_ref, o_ref):
    x = x_ref[...]
    h = _rms(x, gpre_ref[...]).astype(BF16)
    y = None
    for j in range(wg_ref.shape[1] // tf):
        cols = slice(j * tf, (j + 1) * tf)
        g = _dot(h, wg_ref[:, cols])
        u = _dot(h, wu_ref[:, cols])
        a = (g * jax.nn.sigmoid(g) * u).astype(BF16)
        part = _dot(a, wd_ref[cols, :])
        y = part if y is None else y + part
    o_ref[...] = x + 0.5 * _rms(y, gpost_ref[...])


def _ffn(x, gpre, gpost, wg, wu, wd, layer, slot, *, tm, tf):
    t, d = x.shape
    f = wg.shape[-1]
    resident = lambda r, c: pl.BlockSpec((None, None, r, c), lambda i: (layer, slot, 0, 0),
                                         pipeline_mode=pl.Buffered(1))
    return pl.pallas_call(
        functools.partial(_ffn_body, tf),
        grid=(t // tm,),
        in_specs=[pl.BlockSpec((tm, d), lambda i: (i, 0)), _full((1, d)), _full((1, d)),
                  resident(d, f), resident(d, f), resident(f, d)],
        out_specs=pl.BlockSpec((tm, d), lambda i: (i, 0)),
        out_shape=jax.ShapeDtypeStruct((t, d), F32),
        compiler_params=_params("parallel"),
        name="ffn",
    )(x, gpre, gpost, wg, wu, wd)


def _proj_body(n_in, x_ref, g_ref, *refs):
    a_refs, w_refs, o_ref = refs[:n_in], refs[n_in:2 * n_in], refs[2 * n_in]
    y = _dot(a_refs[0][...], w_refs[0][...])
    for a_ref, w_ref in zip(a_refs[1:], w_refs[1:]):
        y += _dot(a_ref[...], w_ref[...])
    o_ref[...] = x_ref[...] + _rms(y, g_ref[...])


def _proj(x, g, acts, weights, *, tm):
    t, d = x.shape
    n_in = len(acts)
    return pl.pallas_call(
        functools.partial(_proj_body, n_in),
        grid=(t // tm,),
        in_specs=[pl.BlockSpec((tm, d), lambda i: (i, 0)), _full((1, d))]
        + [pl.BlockSpec((tm, a.shape[1]), lambda i: (i, 0)) for a in acts]
        + [_full(w.shape) for w in weights],
        out_specs=pl.BlockSpec((tm, d), lambda i: (i, 0)),
        out_shape=jax.ShapeDtypeStruct((t, d), F32),
        compiler_params=_params("parallel"),
        name="mixer_out_proj",
    )(x, g, *acts, *weights)


def _rope_tables(pos):
    half = A_ROPE // 2
    lane = lax.broadcasted_iota(jnp.int32, (1, LANES), 1)
    j = (lane % half).astype(F32)
    inv = jnp.exp(j * (-math.log(ROPE_BASE) / half))
    ang = pos * inv
    sin = jnp.where(lane < A_ROPE, jnp.sin(ang), 0.0)
    return jnp.cos(ang), sin


def _rope_apply(r, cos, sin):
    half = A_ROPE // 2
    return r * cos + (pltpu.roll(r, half, axis=1) - pltpu.roll(r, LANES - half, axis=1)) * sin


AB_Z = A_Q_LORA + A_KV_LORA + LANES + 3 * B_WIDTH
A_HEAD_PAD = 2 * LANES


def _ab_prep_body(x_ref, pos_ref, g_ref, w_in_ref, cq_g_ref, ckv_g_ref, w_uq_ref, w_ukv_ref,
                  q_ref, k_ref, v_ref, xm_ref, vm_ref, og_ref):
    h = _rms(x_ref[...], g_ref[...]).astype(BF16)
    z = _dot(h, w_in_ref[...])
    o = 0
    cq = z[:, o:o + A_Q_LORA]; o += A_Q_LORA
    ckv = z[:, o:o + A_KV_LORA]; o += A_KV_LORA
    kr = z[:, o:o + LANES]; o += LANES
    xm_ref[...] = z[:, o:o + B_WIDTH]; o += B_WIDTH
    vm_ref[...] = z[:, o:o + B_WIDTH]; o += B_WIDTH
    og_ref[...] = z[:, o:o + B_WIDTH]

    cos, sin = _rope_tables(pos_ref[...])
    scale = (A_NOPE + A_ROPE) ** -0.5
    q = _dot(_rms(cq, cq_g_ref[...]).astype(BF16), w_uq_ref[...]) * scale
    kv = _dot(_rms(ckv, ckv_g_ref[...]).astype(BF16), w_ukv_ref[...])
    k_rope = _rope_apply(kr, cos, sin).astype(BF16)
    for hd in range(A_HEADS):
        base = hd * A_HEAD_PAD
        q_ref[:, base:base + A_NOPE] = q[:, base:base + A_NOPE].astype(BF16)
        q_ref[:, base + A_NOPE:base + A_HEAD_PAD] = _rope_apply(
            q[:, base + A_NOPE:base + A_HEAD_PAD], cos, sin).astype(BF16)
        k_ref[:, base:base + A_NOPE] = kv[:, hd * A_NOPE:(hd + 1) * A_NOPE].astype(BF16)
        k_ref[:, base + A_NOPE:base + A_HEAD_PAD] = k_rope
    v_ref[...] = kv[:, A_HEADS * A_NOPE:].astype(BF16)


def _ab_prep(x, pos, g, w_in, cq_g, ckv_g, w_uq, w_ukv, *, tm):
    t, d = x.shape
    row = lambda w: pl.BlockSpec((tm, w), lambda i: (i, 0))
    return pl.pallas_call(
        _ab_prep_body,
        grid=(t // tm,),
        in_specs=[row(d), row(1), _full((1, d)), _full(w_in.shape), _full(cq_g.shape), _full(ckv_g.shape),
                  _full(w_uq.shape), _full(w_ukv.shape)],
        out_specs=[row(A_HEADS * A_HEAD_PAD), row(A_HEADS * A_HEAD_PAD), row(A_HEADS * A_V),
                   row(B_WIDTH), row(B_WIDTH), row(B_WIDTH)],
        out_shape=[jax.ShapeDtypeStruct((t, A_HEADS * A_HEAD_PAD), BF16),
                   jax.ShapeDtypeStruct((t, A_HEADS * A_HEAD_PAD), BF16),
                   jax.ShapeDtypeStruct((t, A_HEADS * A_V), BF16),
                   jax.ShapeDtypeStruct((t, B_WIDTH), F32),
                   jax.ShapeDtypeStruct((t, B_WIDTH), F32),
                   jax.ShapeDtypeStruct((t, B_WIDTH), F32)],
        compiler_params=_params("parallel"),
        name="ab_prep",
    )(x, pos, g, w_in, cq_g, ckv_g, w_uq, w_ukv)


def _mla_attn_body(tq, q_ref, k_ref, v_ref, o_ref, m_scr, l_scr, acc_scr):
    i = pl.program_id(1)
    heads = range(A_HEADS)
    m_scr[...] = jnp.full(m_scr.shape, -jnp.inf, F32)
    l_scr[...] = jnp.zeros(l_scr.shape, F32)
    acc_scr[...] = jnp.zeros(acc_scr.shape, F32)
    widen = lambda a: jnp.concatenate([a] * (tq // LANES), axis=1)

    def chunk(j, masked):
        start = pl.multiple_of(j * tq, tq)
        for h in heads:
            s = _dot_nt(q_ref[:, h * A_HEAD_PAD:(h + 1) * A_HEAD_PAD],
                        k_ref[pl.ds(start, tq), h * A_HEAD_PAD:(h + 1) * A_HEAD_PAD])
            if masked:
                row = lax.broadcasted_iota(jnp.int32, (tq, tq), 0)
                col = lax.broadcasted_iota(jnp.int32, (tq, tq), 1)
                s = jnp.where(col <= row, s, -jnp.inf)
            m_old = m_scr[h]
            m_new = jnp.maximum(m_old, jnp.max(s, axis=-1, keepdims=True))
            alpha = jnp.exp(m_old - m_new)
            p = jnp.exp(s - widen(m_new))
            l_scr[h] = alpha * l_scr[h] + jnp.sum(p, axis=-1, keepdims=True)
            acc_scr[h] = alpha * acc_scr[h] + _dot(p.astype(BF16), v_ref[pl.ds(start, tq), h * A_V:(h + 1) * A_V])
            m_scr[h] = m_new

    def full_chunk(j, carry):
        chunk(j, False)
        return carry

    lax.fori_loop(0, i, full_chunk, 0)
    chunk(i, True)
    for h in heads:
        o_ref[:, h * A_V:(h + 1) * A_V] = (acc_scr[h] / l_scr[h]).astype(BF16)


def _mla_attn(q, k, v, batch, *, tq):
    t = q.shape[0]
    s = t // batch
    nq = s // tq
    return pl.pallas_call(
        functools.partial(_mla_attn_body, tq),
        grid=(batch, nq),
        in_specs=[pl.BlockSpec((tq, A_HEADS * A_HEAD_PAD), lambda b, i: (b * nq + i, 0)),
                  pl.BlockSpec((s, A_HEADS * A_HEAD_PAD), lambda b, i: (b, 0)),
                  pl.BlockSpec((s, A_HEADS * A_V), lambda b, i: (b, 0))],
        out_specs=pl.BlockSpec((tq, A_HEADS * A_V), lambda b, i: (b * nq + i, 0)),
        out_shape=jax.ShapeDtypeStruct((t, A_HEADS * A_V), BF16),
        scratch_shapes=[pltpu.VMEM((A_HEADS, tq, LANES), F32), pltpu.VMEM((A_HEADS, tq, LANES), F32),
                        pltpu.VMEM((A_HEADS, tq, A_V), F32)],
        compiler_params=_params("parallel", "arbitrary"),
        name="mla_attn",
    )(q, k, v)


def _log_sigmoid(x):
    return jnp.minimum(x, 0.0) - jnp.log(1.0 + jnp.exp(-jnp.abs(x)))


def _mlstm_body(xm_ref, vm_ref, og_ref, conv_w_ref, conv_b_ref, wq_ref, wk_ref, wgt_ref, bgt_ref,
                gn_ref, skip_ref, y_ref, xext_scr, gin_scr, state_scr, m_scr):
    c = pl.program_id(1)
    L = B_CHUNK
    tail = 8

    @pl.when(c == 0)
    def _():
        xext_scr[:, 0:tail, :] = jnp.zeros((xext_scr.shape[0], tail, B_WIDTH), F32)
        state_scr[...] = jnp.zeros_like(state_scr)
        m_scr[...] = jnp.zeros_like(m_scr)

    lane = lax.broadcasted_iota(jnp.int32, (L, LANES), 1)
    row = lax.broadcasted_iota(jnp.int32, (L, LANES), 0)
    is_f = (lane >= B_HEADS) & (lane < 2 * B_HEADS)
    causal = lax.broadcasted_iota(jnp.int32, (L, L), 1) <= lax.broadcasted_iota(jnp.int32, (L, L), 0)
    ones_col = jnp.where(lane == 0, 1.0, 0.0).astype(BF16)
    for e in range(xm_ref.shape[0]):
        _mlstm_chunk(xm_ref.at[e], vm_ref.at[e], og_ref.at[e], conv_w_ref, conv_b_ref, wq_ref, wk_ref, wgt_ref,
                     bgt_ref, gn_ref, skip_ref, y_ref.at[e], xext_scr.at[e], gin_scr.at[e], state_scr.at[e],
                     m_scr.at[e], row, is_f, causal, ones_col)


def _mlstm_chunk(xm_ref, vm_ref, og_ref, conv_w_ref, conv_b_ref, wq_ref, wk_ref, wgt_ref, bgt_ref,
                 gn_ref, skip_ref, y_ref, xext_scr, gin_scr, state_scr, m_scr, row, is_f, causal, ones_col):
    L = B_CHUNK
    tail = 8
    xm = xm_ref[...]
    xext_scr[tail:tail + L, :] = xm
    conv = conv_b_ref[...] + conv_w_ref[B_CONV - 1:B_CONV, :] * xm
    for tap in range(B_CONV - 1):
        shift = B_CONV - 1 - tap
        conv += conv_w_ref[tap:tap + 1, :] * xext_scr[tail - shift:tail - shift + L, :]
    xext_scr[0:tail, :] = xm[L - tail:, :]
    xc = conv * jax.nn.sigmoid(conv)
    xcb = xc.astype(BF16)

    vm = vm_ref[...]
    qs, ks = [], []
    for h in range(B_HEADS):
        sl = slice(h * B_HEAD_DIM, (h + 1) * B_HEAD_DIM)
        qh = _dot(xcb[:, sl], wq_ref[h])
        kh = _dot(xcb[:, sl], wk_ref[h])
        qs.append(qh)
        ks.append(kh)
        gin_scr[:, h * B_HEAD_DIM:(h + 1) * B_HEAD_DIM] = qh.astype(BF16)
        gin_scr[:, B_WIDTH + h * B_HEAD_DIM:B_WIDTH + (h + 1) * B_HEAD_DIM] = kh.astype(BF16)
    gin_scr[:, 2 * B_WIDTH:] = vm.astype(BF16)
    gates = _dot(gin_scr[...], wgt_ref[...]) + bgt_ref[...]

    glog = jnp.where(is_f, _log_sigmoid(gates), gates)
    bcum = jnp.where(is_f, glog, 0.0)
    step = 1
    while step < L:
        bcum = bcum + jnp.where(row >= step, pltpu.roll(bcum, step, axis=0), 0.0)
        step *= 2
    glog_t = glog.T
    bcum_t = bcum.T

    kscale = B_HEAD_DIM ** -0.5
    for h in range(B_HEADS):
        sl = slice(h * B_HEAD_DIM, (h + 1) * B_HEAD_DIM)
        qb = qs[h].astype(BF16)
        k_s = ks[h] * kscale
        v_ext = jnp.concatenate([vm[:, sl].astype(BF16), ones_col], axis=1)
        li_c = glog[:, h:h + 1]
        b_c = bcum[:, B_HEADS + h:B_HEADS + h + 1]
        li_r = glog_t[h:h + 1, :]
        b_r = bcum_t[B_HEADS + h:B_HEADS + h + 1, :]
        b_tot = b_r[:, L - 1:L]
        m_st = m_scr[h:h + 1, 0:1]
        state = state_scr[h]

        dmat = jnp.where(causal, b_c - b_r + li_r, -jnp.inf)
        m_inter = b_c + m_st
        m_j = jnp.maximum(m_inter, jnp.max(dmat, axis=-1, keepdims=True))
        sc = _dot_nt(qb, k_s.astype(BF16)) * jnp.exp(dmat - m_j)
        inter = jnp.exp(m_inter - m_j)
        ne = inter * _dot(qb, state.astype(BF16)) + _dot(sc.astype(BF16), v_ext)
        den = ne[:, B_HEAD_DIM:B_HEAD_DIM + 1]
        hj = ne[:, :B_HEAD_DIM] / jnp.maximum(jnp.abs(den), jnp.exp(-m_j))

        g_c = b_tot - b_c + li_c
        g_r = b_tot - b_r + li_r
        m_new = jnp.maximum(b_tot + m_st, jnp.max(g_r, axis=-1, keepdims=True))
        decay = jnp.exp(b_tot + m_st - m_new)
        kw = (k_s * jnp.exp(g_c - m_new)).astype(BF16)
        state_scr[h] = decay * state + _dot_tn(kw, v_ext)
        m_scr[h:h + 1, :] = jnp.broadcast_to(m_new, (1, LANES))

        mu = jnp.mean(hj, axis=-1, keepdims=True)
        var = jnp.mean(jnp.square(hj - mu), axis=-1, keepdims=True)
        hn = (hj - mu) * lax.rsqrt(var + EPS)
        out = (hn * gn_ref[:, sl] + skip_ref[:, sl] * xc[:, sl]) * jax.nn.sigmoid(og_ref[:, sl])
        y_ref[:, sl] = out.astype(BF16)


def _mlstm(xm, vm, og, conv_w, conv_b, wq, wk, wgt, bgt, gn_g, skip, batch):
    t = xm.shape[0]
    s = t // batch
    per_step = 2 if batch % 2 == 0 else 1
    seq3 = lambda a: a.reshape(batch, s, B_WIDTH)
    row = pl.BlockSpec((per_step, B_CHUNK, B_WIDTH), lambda b, c: (b, c, 0))
    y = pl.pallas_call(
        _mlstm_body,
        grid=(batch // per_step, s // B_CHUNK),
        in_specs=[row, row, row, _full(conv_w.shape), _full(conv_b.shape), _full(wq.shape), _full(wk.shape),
                  _full(wgt.shape), _full(bgt.shape), _full(gn_g.shape), _full(skip.shape)],
        out_specs=row,
        out_shape=jax.ShapeDtypeStruct((batch, s, B_WIDTH), BF16),
        scratch_shapes=[pltpu.VMEM((per_step, 8 + B_CHUNK, B_WIDTH), F32),
                        pltpu.VMEM((per_step, B_CHUNK, 3 * B_WIDTH), BF16),
                        pltpu.VMEM((per_step, B_HEADS, B_HEAD_DIM, 2 * LANES), F32),
                        pltpu.VMEM((per_step, 8, LANES), F32)],
        compiler_params=_params("parallel", "arbitrary"),
        name="mlstm",
    )(seq3(xm), seq3(vm), seq3(og), conv_w, conv_b, wq, wk, wgt, bgt, gn_g, skip)
    return y.reshape(t, B_WIDTH)


C_Z = C_Q_LORA + C_KV_LORA + IDX_HEADS * LANES + LANES + LANES


def _dsa_prep_body(x_ref, g_ref, w_in_ref, cq_g_ref, ckv_g_ref, w_uq_ref, w_ukt_ref,
                   qlat_ref, ckv_ref, qi_ref, ki_ref, wi_ref):
    h = _rms(x_ref[...], g_ref[...]).astype(BF16)
    z = _dot(h, w_in_ref[...])
    o = 0
    cq = z[:, o:o + C_Q_LORA]; o += C_Q_LORA
    ckv = z[:, o:o + C_KV_LORA]; o += C_KV_LORA
    qi_ref[...] = z[:, o:o + IDX_HEADS * LANES].astype(BF16); o += IDX_HEADS * LANES
    ki_ref[...] = z[:, o:o + LANES].astype(BF16); o += LANES
    wi_ref[...] = z[:, o:o + LANES]
    ckv_ref[...] = _rms(ckv, ckv_g_ref[...]).astype(BF16)
    q = _dot(_rms(cq, cq_g_ref[...]).astype(BF16), w_uq_ref[...]).astype(BF16)
    scale = C_NOPE ** -0.5
    for hd in range(C_HEADS):
        ql = _dot(q[:, hd * C_NOPE:(hd + 1) * C_NOPE], w_ukt_ref[hd]) * scale
        qlat_ref[:, hd * C_KV_LORA:(hd + 1) * C_KV_LORA] = ql.astype(BF16)


def _dsa_prep(x, g, w_in, cq_g, ckv_g, w_uq, w_ukt, *, tm):
    t, d = x.shape
    row = lambda w: pl.BlockSpec((tm, w), lambda i: (i, 0))
    return pl.pallas_call(
        _dsa_prep_body,
        grid=(t // tm,),
        in_specs=[row(d), _full((1, d)), _full(w_in.shape), _full(cq_g.shape), _full(ckv_g.shape),
                  _full(w_uq.shape), _full(w_ukt.shape)],
        out_specs=[row(C_HEADS * C_KV_LORA), row(C_KV_LORA), row(IDX_HEADS * LANES), row(LANES), row(LANES)],
        out_shape=[jax.ShapeDtypeStruct((t, C_HEADS * C_KV_LORA), BF16),
                   jax.ShapeDtypeStruct((t, C_KV_LORA), BF16),
                   jax.ShapeDtypeStruct((t, IDX_HEADS * LANES), BF16),
                   jax.ShapeDtypeStruct((t, LANES), BF16),
                   jax.ShapeDtypeStruct((t, LANES), F32)],
        compiler_params=_params("parallel"),
        name="dsa_prep",
    )(x, g, w_in, cq_g, ckv_g, w_uq, w_ukt)


INT_MAX = 2 ** 31 - 1
DSA_VMEM_LIMIT = 56 * 1024 * 1024


def _key_to_f32(key):
    return pltpu.bitcast(key ^ ((key >> 31) & 0x7FFFFFFF), F32)


def _dsa_attn_body(topk, seq, qi_ref, wi_ref, qlat_ref, ki_ref, ckv_ref, w_uv_ref, o_ref,
                   score_scr, key_scr, cnt_scr, bound_scr, m_scr, l_scr, acc_scr):
    Q = Q_BLOCK
    H = C_HEADS
    nb = seq // Q
    first = topk // Q
    col_iota = lax.broadcasted_iota(jnp.int32, (Q, Q), 1)
    ones = jnp.ones((Q, LANES), BF16)
    topk_f = float(topk)

    qa = 2 * Q
    qa_col = lax.broadcasted_iota(jnp.int32, (qa, qa), 1)
    qa_row = lax.broadcasted_iota(jnp.int32, (qa, qa), 0)

    def score_block(blk, _):
        r0 = pl.multiple_of(blk * qa, qa)
        qi = jnp.concatenate([qi_ref[pl.ds(r0, qa), h * LANES:(h + 1) * LANES] for h in range(IDX_HEADS)], axis=0)
        wi = wi_ref[pl.ds(r0, qa), :]
        wb = [jnp.broadcast_to(wi[:, h:h + 1], (qa, qa)) for h in range(IDX_HEADS)]

        def tile(c, _):
            k0 = pl.multiple_of(c * qa, qa)
            rel = jnp.maximum(_dot_nt(qi, ki_ref[pl.ds(k0, qa), :]), 0.0)
            score = rel[0:qa] * wb[0]
            for h in range(1, IDX_HEADS):
                score += rel[h * qa:(h + 1) * qa] * wb[h]
            score = jnp.where(k0 + qa_col <= r0 + qa_row, score, -jnp.inf)
            score_scr[pl.ds(r0, qa), pl.ds(k0, qa)] = score
            return 0

        lax.fori_loop(0, blk + 1, tile, 0)
        return 0

    lax.fori_loop(0, seq // qa, score_block, 0)

    if first > 0:
        key_scr[0:first * Q, :] = jnp.full((first * Q, LANES), KEY_LOWEST_FINITE, jnp.int32)
        cnt_scr[0:first * Q, :] = jnp.full((first * Q, LANES), topk_f, F32)
    for i in range(first, nb):
        key_scr[i * Q:(i + 1) * Q, :] = jnp.full((Q, LANES), INT_MIN, jnp.int32)
        cnt_scr[i * Q:(i + 1) * Q, :] = jnp.full((Q, LANES), float((i + 1) * Q), F32)

    def bit_step(b, _):
        bit = jnp.left_shift(jnp.int32(1), 31 - b)
        for i in range(first, nb):
            rows = slice(i * Q, (i + 1) * Q)
            key = key_scr[rows, :]
            cand = jnp.where(b == 0, jnp.zeros_like(key), key | bit)
            cand_f = _key_to_f32(cand)
            hits = jnp.zeros((Q, LANES), F32)
            for c in range(i + 1):
                hits += jnp.where(score_scr[rows, c * Q:(c + 1) * Q] >= cand_f, 1.0, 0.0)
            cnt = _dot(hits.astype(BF16), ones)
            ok = cnt >= topk_f
            key_scr[rows, :] = jnp.where(ok, cand, key)
            cnt_scr[rows, :] = jnp.where(ok, cnt, cnt_scr[rows, :])
        return 0

    lax.fori_loop(0, 32, bit_step, 0)

    bound_scr[...] = jnp.full(bound_scr.shape, INT_MAX, jnp.int32)

    idx_bits = (seq - 1).bit_length()

    def tie_block(i, _):
        r0 = pl.multiple_of(i * Q, Q)

        @pl.when(jnp.max(cnt_scr[pl.ds(r0, Q), :]) > topk_f)
        def _():
            thr = _key_to_f32(key_scr[pl.ds(r0, Q), :])

            def count(pred_fn):
                def body(c, hits):
                    k0 = pl.multiple_of(c * Q, Q)
                    return hits + jnp.where(pred_fn(score_scr[pl.ds(r0, Q), pl.ds(k0, Q)], k0), 1.0, 0.0)
                return _dot(lax.fori_loop(0, i + 1, body, jnp.zeros((Q, LANES), F32)).astype(BF16), ones)

            need = topk_f - count(lambda sc, k0: sc > thr)

            def pos_step(b, bound):
                cand = bound - jnp.right_shift(jnp.int32(1 << (idx_bits - 1)), b)
                cnt = count(lambda sc, k0: (sc == thr) & (k0 + col_iota <= cand))
                return jnp.where(cnt >= need, cand, bound)

            all_keys = jnp.full((Q, LANES), (1 << idx_bits) - 1, jnp.int32)
            bound_scr[pl.ds(r0, Q), :] = lax.fori_loop(0, idx_bits, pos_step, all_keys)
        return 0

    lax.fori_loop(first, nb, tie_block, 0)

    twice = lambda a: jnp.concatenate([a, a], axis=1)

    def attn_block(blk, _):
        r0 = pl.multiple_of(blk * qa, qa)
        qlat = jnp.concatenate([qlat_ref[pl.ds(r0, qa), h * C_KV_LORA:(h + 1) * C_KV_LORA] for h in range(H)], axis=0)
        thr = _key_to_f32(key_scr[pl.ds(r0, qa), :])
        bound = bound_scr[pl.ds(r0, qa), :]
        m_scr[...] = jnp.full(m_scr.shape, -jnp.inf, F32)
        l_scr[...] = jnp.zeros(l_scr.shape, F32)
        acc_scr[...] = jnp.zeros(acc_scr.shape, F32)

        def tile(k0, width):
            wide = lambda a: jnp.concatenate([a] * (width // LANES), axis=1)
            sc = score_scr[pl.ds(r0, qa), pl.ds(k0, width)]
            kidx = k0 + lax.broadcasted_iota(jnp.int32, (qa, width), 1)
            thr_w = wide(thr)
            sel = (sc > thr_w) | ((sc == thr_w) & (kidx <= wide(bound)))
            ckv = ckv_ref[pl.ds(k0, width), :]
            logits = _dot_nt(qlat, ckv)
            for h in range(H):
                rs = slice(h * qa, (h + 1) * qa)
                s = jnp.where(sel, logits[rs], -jnp.inf)
                m_old = m_scr[rs]
                m_new = jnp.maximum(m_old, jnp.max(s, axis=-1, keepdims=True))
                m_safe = jnp.where(m_new == -jnp.inf, 0.0, m_new)
                alpha = jnp.exp(m_old - m_safe)
                p = jnp.exp(s - wide(m_safe))
                l_scr[rs] = alpha * l_scr[rs] + jnp.sum(p, axis=-1, keepdims=True)
                acc_scr[rs] = twice(alpha) * acc_scr[rs] + _dot(p.astype(BF16), ckv)
                m_scr[rs] = m_new

        def tile_pair(cp, carry):
            tile(pl.multiple_of(cp * 2 * qa, 2 * qa), 2 * qa)
            return carry

        lax.fori_loop(0, (blk + 1) // 2, tile_pair, 0)

        @pl.when(blk % 2 == 0)
        def _():
            tile(pl.multiple_of(blk * qa, qa), qa)

        for h in range(H):
            rs = slice(h * qa, (h + 1) * qa)
            o_lat = (acc_scr[rs] / twice(l_scr[rs])).astype(BF16)
            o_ref[pl.ds(r0, qa), h * C_V:(h + 1) * C_V] = _dot(o_lat, w_uv_ref[h]).astype(BF16)
        return 0

    lax.fori_loop(0, seq // qa, attn_block, 0)


def _dsa_attn(qi, wi, qlat, ki, ckv, w_uv, batch):
    t = qi.shape[0]
    s = t // batch
    topk = min(TOPK_MAX, s // 4)
    assert topk % Q_BLOCK == 0 and s % (2 * Q_BLOCK) == 0 and s // Q_BLOCK <= 256
    seq = lambda w: pl.BlockSpec((s, w), lambda b: (b, 0), pipeline_mode=pl.Buffered(1))
    qa = 2 * Q_BLOCK
    return pl.pallas_call(
        functools.partial(_dsa_attn_body, topk, s),
        grid=(batch,),
        in_specs=[seq(IDX_HEADS * LANES), seq(LANES), seq(C_HEADS * C_KV_LORA), seq(LANES), seq(C_KV_LORA),
                  _full(w_uv.shape)],
        out_specs=pl.BlockSpec((s, C_HEADS * C_V), lambda b: (b, 0)),
        out_shape=jax.ShapeDtypeStruct((t, C_HEADS * C_V), BF16),
        scratch_shapes=[pltpu.VMEM((s, s), F32),
                        pltpu.VMEM((s, LANES), jnp.int32),
                        pltpu.VMEM((s, LANES), F32),
                        pltpu.VMEM((s, LANES), jnp.int32),
                        pltpu.VMEM((C_HEADS * qa, LANES), F32),
                        pltpu.VMEM((C_HEADS * qa, LANES), F32),
                        pltpu.VMEM((C_HEADS * qa, C_KV_LORA), F32)],
        compiler_params=pltpu.CompilerParams(dimension_semantics=("arbitrary",), vmem_limit_bytes=DSA_VMEM_LIMIT),
        name="dsa_attn",
    )(qi, wi, qlat, ki, ckv, w_uv)


def _mem_kv_body(mem_ref, g_ref, wk_ref, wv_ref, k_ref, v_ref):
    mn = _rms(mem_ref[...], g_ref[...]).astype(BF16)
    k_ref[...] = _dot(mn, wk_ref[...]).astype(BF16)
    v_ref[...] = _dot(mn, wv_ref[...]).astype(BF16)


def _mem_kv(mem, g, wk, wv, layer, n_mem):
    rows, d = mem.shape
    w = wk.shape[-1]
    wspec = pl.BlockSpec((None, d, w), lambda b: (layer, 0, 0))
    return pl.pallas_call(
        _mem_kv_body,
        grid=(rows // n_mem,),
        in_specs=[pl.BlockSpec((n_mem, d), lambda b: (b, 0)), _full((1, d)), wspec, wspec],
        out_specs=[pl.BlockSpec((n_mem, w), lambda b: (b, 0))] * 2,
        out_shape=[jax.ShapeDtypeStruct((rows, w), BF16)] * 2,
        compiler_params=_params("parallel"),
        name="mem_kv",
    )(mem, g, wk, wv)


def _xattn_body(x_ref, gpre_ref, gpost_ref, wq_ref, wo_ref, k_ref, v_ref, o_ref, att_scr):
    x = x_ref[...]
    h = _rms(x, gpre_ref[...]).astype(BF16)
    q = (_dot(h, wq_ref[...]) * X_HEAD_DIM ** -0.5).astype(BF16)
    for hd in range(X_HEADS):
        sl = slice(hd * X_HEAD_DIM, (hd + 1) * X_HEAD_DIM)
        s = _dot_nt(q[:, sl], k_ref[:, sl])
        p = jnp.exp(s - jnp.max(s, axis=-1, keepdims=True))
        o = _dot(p.astype(BF16), v_ref[:, sl]) / jnp.sum(p, axis=-1, keepdims=True)
        att_scr[:, sl] = o.astype(BF16)
    y = _dot(att_scr[...], wo_ref[...])
    o_ref[...] = x + _rms(y, gpost_ref[...])


def _xattn(x, gpre, gpost, wq, wo, k, v, layer, batch, *, tm):
    t, d = x.shape
    w = wq.shape[-1]
    n_mem = k.shape[0] // batch
    tiles_per_batch = t // batch // tm
    return pl.pallas_call(
        _xattn_body,
        grid=(t // tm,),
        in_specs=[pl.BlockSpec((tm, d), lambda i: (i, 0)), _full((1, d)), _full((1, d)),
                  pl.BlockSpec((None, d, w), lambda i: (layer, 0, 0)),
                  pl.BlockSpec((None, w, d), lambda i: (layer, 0, 0)),
                  pl.BlockSpec((n_mem, w), lambda i: (i // tiles_per_batch, 0)),
                  pl.BlockSpec((n_mem, w), lambda i: (i // tiles_per_batch, 0))],
        out_specs=pl.BlockSpec((tm, d), lambda i: (i, 0)),
        out_shape=jax.ShapeDtypeStruct((t, d), F32),
        scratch_shapes=[pltpu.VMEM((tm, w), BF16)],
        compiler_params=_params("parallel"),
        name="xattn",
    )(x, gpre, gpost, wq, wo, k, v)


def _pad_cols(w, width):
    return jnp.pad(w, ((0, 0), (0, width - w.shape[1])))


def _prep_ab_weights(w_in, w_uq, w_ukv, w_gates, b_gates):
    o = A_Q_LORA + A_KV_LORA
    w_in_p = jnp.concatenate([w_in[:, :o], _pad_cols(w_in[:, o:o + A_ROPE], LANES), w_in[:, o + A_ROPE:]], axis=1)
    uq = w_uq.reshape(A_Q_LORA, A_HEADS, A_NOPE + A_ROPE)
    uq = jnp.pad(uq, ((0, 0), (0, 0), (0, A_HEAD_PAD - A_NOPE - A_ROPE))).reshape(A_Q_LORA, A_HEADS * A_HEAD_PAD)
    ukv = w_ukv.reshape(A_KV_LORA, A_HEADS, A_NOPE + A_V)
    ukv = jnp.concatenate([ukv[:, :, :A_NOPE].reshape(A_KV_LORA, -1), ukv[:, :, A_NOPE:].reshape(A_KV_LORA, -1)], axis=1)
    return (w_in_p.astype(BF16), uq.astype(BF16), ukv.astype(BF16),
            _pad_cols(w_gates, LANES).astype(BF16), _pad_cols(b_gates[None, :], LANES))


def _prep_c_weights(w_in, w_uk):
    o = C_Q_LORA + C_KV_LORA
    qi = w_in[:, o:o + IDX_HEADS * IDX_DIM].reshape(-1, IDX_HEADS, IDX_DIM)
    qi = jnp.pad(qi, ((0, 0), (0, 0), (0, LANES - IDX_DIM))).reshape(-1, IDX_HEADS * LANES)
    o2 = o + IDX_HEADS * IDX_DIM
    w_in_p = jnp.concatenate([w_in[:, :o], qi, _pad_cols(w_in[:, o2:o2 + IDX_DIM], LANES),
                              _pad_cols(w_in[:, o2 + IDX_DIM:], LANES)], axis=1)
    return w_in_p.astype(BF16), jnp.swapaxes(w_uk, 1, 2).astype(BF16)


def kernel(x, mem, positions, norm_g, mem_norm_g, ffn_w_gate, ffn_w_up, ffn_w_down, xa_wq, xa_wk, xa_wv, xa_wo, ab_w_in, ab_w_out, mla_cq_g, mla_ckv_g, mla_w_uq, mla_w_ukv, ml_conv_w, ml_conv_b, ml_wq, ml_wk, ml_w_gates, ml_b_gates, ml_gn_g, ml_skip, c_w_in, c_w_out, c_cq_g, c_ckv_g, c_w_uq, c_w_uk, c_w_uv):
    batch, seq, d = x.shape
    depth = norm_g.shape[0]
    n_mem = mem.shape[1]
    t = batch * seq
    tm = min(512, seq)
    ffn_dim = ffn_w_gate.shape[-1]
    tf = ffn_dim // 2 if (ffn_dim // 2) % LANES == 0 else ffn_dim

    xt = x.reshape(t, d)
    memt = mem.reshape(batch * n_mem, d)
    pos = positions.reshape(t, 1).astype(F32)
    wg, wu, wd = ffn_w_gate.astype(BF16), ffn_w_up.astype(BF16), ffn_w_down.astype(BF16)
    xa_wq_b, xa_wk_b, xa_wv_b, xa_wo_b = (w.astype(BF16) for w in (xa_wq, xa_wk, xa_wv, xa_wo))

    for l in range(depth):
        g = norm_g[l][:, None, :]
        xt = _ffn(xt, g[0], g[1], wg, wu, wd, l, 0, tm=tm, tf=tf)
        if l % 2 == 0:
            e = l // 2
            w_in_p, uq, ukv, wgt, bgt = _prep_ab_weights(ab_w_in[e], mla_w_uq[e], mla_w_ukv[e],
                                                         ml_w_gates[e], ml_b_gates[e])
            q, k, v, xm, vm, og = _ab_prep(xt, pos, g[2], w_in_p, mla_cq_g[e][None], mla_ckv_g[e][None], uq, ukv, tm=tm)
            ya = _mla_attn(q, k, v, batch, tq=min(256, seq))
            yb = _mlstm(xm, vm, og, ml_conv_w[e], ml_conv_b[e][None], ml_wq[e].astype(BF16), ml_wk[e].astype(BF16),
                        wgt, bgt, ml_gn_g[e][None], ml_skip[e][None], batch)
            w_out = ab_w_out[e].astype(BF16)
            xt = _proj(xt, g[3], [ya, yb], [w_out[:A_HEADS * A_V], w_out[A_HEADS * A_V:]], tm=tm)
        else:
            o = l // 2
            w_in_p, w_ukt = _prep_c_weights(c_w_in[o], c_w_uk[o])
            qlat, ckv, qi, ki, wi = _dsa_prep(xt, g[2], w_in_p, c_cq_g[o][None], c_ckv_g[o][None],
                                              c_w_uq[o].astype(BF16), w_ukt, tm=tm)
            yc = _dsa_attn(qi, wi, qlat, ki, ckv, c_w_uv[o].astype(BF16), batch)
            xt = _proj(xt, g[3], [yc], [c_w_out[o].astype(BF16)], tm=tm)
        mk, mv = _mem_kv(memt, mem_norm_g[l][None], xa_wk_b, xa_wv_b, l, n_mem)
        xt = _xattn(xt, g[4], g[5], xa_wq_b, xa_wo_b, mk, mv, l, batch, tm=tm)
        xt = _ffn(xt, g[6], g[7], wg, wu, wd, l, 1, tm=tm, tf=tf)
    return xt.reshape(batch, seq, d)
```

```python
import functools
import math

import jax
import jax.numpy as jnp
from jax import lax
from jax.experimental import pallas as pl
from jax.experimental.pallas import tpu as pltpu

F32 = jnp.float32
BF16 = jnp.bfloat16

EPS = 1e-6
ROPE_BASE = 10000.0
LANES = 128
VMEM_LIMIT = 48 * 1024 * 1024

A_HEADS, A_Q_LORA, A_KV_LORA, A_NOPE, A_ROPE, A_V = 4, 384, 256, 128, 64, 128
B_HEADS, B_HEAD_DIM, B_CONV, B_CHUNK = 4, 128, 4, 128
B_WIDTH = B_HEADS * B_HEAD_DIM
C_HEADS, C_Q_LORA, C_KV_LORA, C_NOPE, C_V = 8, 384, 256, 128, 128
IDX_HEADS, IDX_DIM, TOPK_MAX = 8, 64, 256
X_HEADS, X_HEAD_DIM = 4, 128
Q_BLOCK = 128

INT_MIN = -(2 ** 31)
KEY_LOWEST_FINITE = INT_MIN + 0x00800000


def _params(*sem):
    return pltpu.CompilerParams(dimension_semantics=sem, vmem_limit_bytes=VMEM_LIMIT)


def _rms(x, g):
    return x * lax.rsqrt(jnp.mean(x * x, axis=-1, keepdims=True) + EPS) * g


def _dot(a, b):
    return jnp.dot(a, b, preferred_element_type=F32)


def _dot_nt(a, b):
    return lax.dot_general(a, b, (((1,), (1,)), ((), ())), preferred_element_type=F32)


def _dot_tn(a, b):
    return lax.dot_general(a, b, (((0,), (0,)), ((), ())), preferred_element_type=F32)


def _full(shape):
    return pl.BlockSpec(shape, lambda *_: (0,) * len(shape))


FUSED_VMEM_LIMIT = 56 * 1024 * 1024
MXU_DIM = 256


def _ffn_chunks(f):
    if f % MXU_DIM:
        return [(0, f)]
    first = (f // MXU_DIM + 1) // 2 * MXU_DIM
    return [(0, first), (first, f)] if first < f else [(0, f)]


def _ffn_val(x, gpre, gpost, wg_ref, wu_ref, wd_ref):
    h = _rms(x, gpre).astype(BF16)
    y = None
    for lo, hi in _ffn_chunks(wg_ref.shape[1]):
        cols = slice(lo, hi)
        g = _dot(h, wg_ref[:, cols])
        u = _dot(h, wu_ref[:, cols])
        a = (g * jax.nn.sigmoid(g) * u).astype(BF16)
        part = _dot(a, wd_ref[cols, :])
        y = part if y is None else y + part
    return x + 0.5 * _rms(y, gpost)


def _rope_tables(pos):
    half = A_ROPE // 2
    lane = lax.broadcasted_iota(jnp.int32, (1, LANES), 1)
    j = (lane % half).astype(F32)
    inv = jnp.exp(j * (-math.log(ROPE_BASE) / half))
    ang = pos * inv
    sin = jnp.where(lane < A_ROPE, jnp.sin(ang), 0.0)
    return jnp.cos(ang), sin


def _rope_apply(r, cos, sin):
    half = A_ROPE // 2
    return r * cos + (pltpu.roll(r, half, axis=1) - pltpu.roll(r, LANES - half, axis=1)) * sin


AB_Z = A_Q_LORA + A_KV_LORA + LANES + 3 * B_WIDTH
A_HEAD_PAD = 2 * LANES


def _ab_prep_val(x, pos, g, w_in_ref, cq_g_ref, ckv_g_ref, w_uq_ref, w_ukv_ref,
                 q_ref, k_ref, v_ref, xm_ref, vm_ref, og_ref):
    h = _rms(x, g).astype(BF16)
    z = _dot(h, w_in_ref[...])
    o = 0
    cq = z[:, o:o + A_Q_LORA]; o += A_Q_LORA
    ckv = z[:, o:o + A_KV_LORA]; o += A_KV_LORA
    kr = z[:, o:o + LANES]; o += LANES
    xm_ref[...] = z[:, o:o + B_WIDTH]; o += B_WIDTH
    vm_ref[...] = z[:, o:o + B_WIDTH]; o += B_WIDTH
    og_ref[...] = z[:, o:o + B_WIDTH]

    cos, sin = _rope_tables(pos)
    scale = (A_NOPE + A_ROPE) ** -0.5
    q = _dot(_rms(cq, cq_g_ref[...]).astype(BF16), w_uq_ref[...]) * scale
    kv = _dot(_rms(ckv, ckv_g_ref[...]).astype(BF16), w_ukv_ref[...])
    k_rope = _rope_apply(kr, cos, sin).astype(BF16)
    for hd in range(A_HEADS):
        base = hd * A_HEAD_PAD
        q_ref[:, base:base + A_NOPE] = q[:, base:base + A_NOPE].astype(BF16)
        q_ref[:, base + A_NOPE:base + A_HEAD_PAD] = _rope_apply(
            q[:, base + A_NOPE:base + A_HEAD_PAD], cos, sin).astype(BF16)
        k_ref[:, base:base + A_NOPE] = kv[:, hd * A_NOPE:(hd + 1) * A_NOPE].astype(BF16)
        k_ref[:, base + A_NOPE:base + A_HEAD_PAD] = k_rope
    v_ref[...] = kv[:, A_HEADS * A_NOPE:].astype(BF16)


def _pre_ab_body(x_ref, pos_ref, g_ref, wg_ref, wu_ref, wd_ref, w_in_ref, cq_g_ref, ckv_g_ref, w_uq_ref,
                 w_ukv_ref, xo_ref, q_ref, k_ref, v_ref, xm_ref, vm_ref, og_ref):
    x = _ffn_val(x_ref[...], g_ref[0:1, :], g_ref[1:2, :], wg_ref, wu_ref, wd_ref)
    xo_ref[...] = x
    _ab_prep_val(x, pos_ref[...], g_ref[2:3, :], w_in_ref, cq_g_ref, ckv_g_ref, w_uq_ref, w_ukv_ref,
                 q_ref, k_ref, v_ref, xm_ref, vm_ref, og_ref)


def _pre_ab(x, pos, g, wg, wu, wd, layer, w_in, cq_g, ckv_g, w_uq, w_ukv, *, tm):
    t, d = x.shape
    f = wg.shape[-1]
    once = dict(pipeline_mode=pl.Buffered(1))
    resident = lambda r, c: pl.BlockSpec((None, None, r, c), lambda i: (layer, 0, 0, 0), **once)
    const = lambda a: pl.BlockSpec(a.shape, lambda i: (0,) * a.ndim, **once)
    row = lambda w: pl.BlockSpec((tm, w), lambda i: (i, 0))
    return pl.pallas_call(
        _pre_ab_body,
        grid=(t // tm,),
        in_specs=[row(d), row(1), _full(g.shape), resident(d, f), resident(d, f), resident(f, d),
                  const(w_in), const(cq_g), const(ckv_g), const(w_uq), const(w_ukv)],
        out_specs=[row(d), row(A_HEADS * A_HEAD_PAD), row(A_HEADS * A_HEAD_PAD), row(A_HEADS * A_V),
                   row(B_WIDTH), row(B_WIDTH), row(B_WIDTH)],
        out_shape=[jax.ShapeDtypeStruct((t, d), F32),
                   jax.ShapeDtypeStruct((t, A_HEADS * A_HEAD_PAD), BF16),
                   jax.ShapeDtypeStruct((t, A_HEADS * A_HEAD_PAD), BF16),
                   jax.ShapeDtypeStruct((t, A_HEADS * A_V), BF16),
                   jax.ShapeDtypeStruct((t, B_WIDTH), F32),
                   jax.ShapeDtypeStruct((t, B_WIDTH), F32),
                   jax.ShapeDtypeStruct((t, B_WIDTH), F32)],
        compiler_params=pltpu.CompilerParams(dimension_semantics=("parallel",), vmem_limit_bytes=FUSED_VMEM_LIMIT),
        name="pre_ab",
    )(x, pos, g, wg, wu, wd, w_in, cq_g, ckv_g, w_uq, w_ukv)


def _mla_attn_body(tq, q_ref, k_ref, v_ref, o_ref, m_scr, l_scr, acc_scr):
    i = pl.program_id(1)
    heads = range(A_HEADS)
    m_scr[...] = jnp.full(m_scr.shape, -jnp.inf, F32)
    l_scr[...] = jnp.zeros(l_scr.shape, F32)
    acc_scr[...] = jnp.zeros(acc_scr.shape, F32)
    widen = lambda a: jnp.concatenate([a] * (tq // LANES), axis=1)

    def chunk(j, masked):
        start = pl.multiple_of(j * tq, tq)
        for h in heads:
            s = _dot_nt(q_ref[:, h * A_HEAD_PAD:(h + 1) * A_HEAD_PAD],
                        k_ref[pl.ds(start, tq), h * A_HEAD_PAD:(h + 1) * A_HEAD_PAD])
            if masked:
                row = lax.broadcasted_iota(jnp.int32, (tq, tq), 0)
                col = lax.broadcasted_iota(jnp.int32, (tq, tq), 1)
                s = jnp.where(col <= row, s, -jnp.inf)
            m_old = m_scr[h]
            m_new = jnp.maximum(m_old, jnp.max(s, axis=-1, keepdims=True))
            alpha = jnp.exp(m_old - m_new)
            p = jnp.exp(s - widen(m_new))
            l_scr[h] = alpha * l_scr[h] + jnp.sum(p, axis=-1, keepdims=True)
            acc_scr[h] = alpha * acc_scr[h] + _dot(p.astype(BF16), v_ref[pl.ds(start, tq), h * A_V:(h + 1) * A_V])
            m_scr[h] = m_new

    def full_chunk(j, carry):
        chunk(j, False)
        return carry

    lax.fori_loop(0, i, full_chunk, 0)
    chunk(i, True)
    for h in heads:
        o_ref[:, h * A_V:(h + 1) * A_V] = (acc_scr[h] / l_scr[h]).astype(BF16)


def _mla_attn(q, k, v, batch, *, tq):
    t = q.shape[0]
    s = t // batch
    nq = s // tq
    return pl.pallas_call(
        functools.partial(_mla_attn_body, tq),
        grid=(batch, nq),
        in_specs=[pl.BlockSpec((tq, A_HEADS * A_HEAD_PAD), lambda b, i: (b * nq + i, 0)),
                  pl.BlockSpec((s, A_HEADS * A_HEAD_PAD), lambda b, i: (b, 0)),
                  pl.BlockSpec((s, A_HEADS * A_V), lambda b, i: (b, 0))],
        out_specs=pl.BlockSpec((tq, A_HEADS * A_V), lambda b, i: (b * nq + i, 0)),
        out_shape=jax.ShapeDtypeStruct((t, A_HEADS * A_V), BF16),
        scratch_shapes=[pltpu.VMEM((A_HEADS, tq, LANES), F32), pltpu.VMEM((A_HEADS, tq, LANES), F32),
                        pltpu.VMEM((A_HEADS, tq, A_V), F32)],
        compiler_params=_params("parallel", "arbitrary"),
        name="mla_attn",
    )(q, k, v)


def _log_sigmoid(x):
    return jnp.minimum(x, 0.0) - jnp.log(1.0 + jnp.exp(-jnp.abs(x)))


def _mlstm_body(xm_ref, vm_ref, og_ref, conv_w_ref, conv_b_ref, wq_ref, wk_ref, wgt_ref, bgt_ref,
                gn_ref, skip_ref, y_ref, xext_scr, gin_scr, state_scr, m_scr):
    c = pl.program_id(1)
    L = B_CHUNK
    tail = 8

    @pl.when(c == 0)
    def _():
        xext_scr[:, 0:tail, :] = jnp.zeros((xext_scr.shape[0], tail, B_WIDTH), F32)
        state_scr[...] = jnp.zeros_like(state_scr)
        m_scr[...] = jnp.zeros_like(m_scr)

    lane = lax.broadcasted_iota(jnp.int32, (L, LANES), 1)
    row = lax.broadcasted_iota(jnp.int32, (L, LANES), 0)
    is_f = (lane >= B_HEADS) & (lane < 2 * B_HEADS)
    causal = lax.broadcasted_iota(jnp.int32, (L, L), 1) <= lax.broadcasted_iota(jnp.int32, (L, L), 0)
    ones_col = jnp.where(lane == 0, 1.0, 0.0).astype(BF16)
    for e in range(xm_ref.shape[0]):
        _mlstm_chunk(xm_ref.at[e], vm_ref.at[e], og_ref.at[e], conv_w_ref, conv_b_ref, wq_ref, wk_ref, wgt_ref,
                     bgt_ref, gn_ref, skip_ref, y_ref.at[e], xext_scr.at[e], gin_scr.at[e], state_scr.at[e],
                     m_scr.at[e], row, is_f, causal, ones_col)


def _mlstm_chunk(xm_ref, vm_ref, og_ref, conv_w_ref, conv_b_ref, wq_ref, wk_ref, wgt_ref, bgt_ref,
                 gn_ref, skip_ref, y_ref, xext_scr, gin_scr, state_scr, m_scr, row, is_f, causal, ones_col):
    L = B_CHUNK
    tail = 8
    xm = xm_ref[...]
    xext_scr[tail:tail + L, :] = xm
    conv = conv_b_ref[...] + conv_w_ref[B_CONV - 1:B_CONV, :] * xm
    for tap in range(B_CONV - 1):
        shift = B_CONV - 1 - tap
        conv += conv_w_ref[tap:tap + 1, :] * xext_scr[tail - shift:tail - shift + L, :]
    xext_scr[0:tail, :] = xm[L - tail:, :]
    xc = conv * jax.nn.sigmoid(conv)
    xcb = xc.astype(BF16)

    vm = vm_ref[...]
    qs, ks = [], []
    for h in range(B_HEADS):
        sl = slice(h * B_HEAD_DIM, (h + 1) * B_HEAD_DIM)
        qh = _dot(xcb[:, sl], wq_ref[h])
        kh = _dot(xcb[:, sl], wk_ref[h])
        qs.append(qh)
        ks.append(kh)
        gin_scr[:, h * B_HEAD_DIM:(h + 1) * B_HEAD_DIM] = qh.astype(BF16)
        gin_scr[:, B_WIDTH + h * B_HEAD_DIM:B_WIDTH + (h + 1) * B_HEAD_DIM] = kh.astype(BF16)
    gin_scr[:, 2 * B_WIDTH:] = vm.astype(BF16)
    gates = _dot(gin_scr[...], wgt_ref[...]) + bgt_ref[...]

    glog = jnp.where(is_f, _log_sigmoid(gates), gates)
    bcum = jnp.where(is_f, glog, 0.0)
    step = 1
    while step < L:
        bcum = bcum + jnp.where(row >= step, pltpu.roll(bcum, step, axis=0), 0.0)
        step *= 2
    glog_t = glog.T
    bcum_t = bcum.T

    kscale = B_HEAD_DIM ** -0.5
    for h in range(B_HEADS):
        sl = slice(h * B_HEAD_DIM, (h + 1) * B_HEAD_DIM)
        qb = qs[h].astype(BF16)
        k_s = ks[h] * kscale
        v_ext = jnp.concatenate([vm[:, sl].astype(BF16), ones_col], axis=1)
        li_c = glog[:, h:h + 1]
        b_c = bcum[:, B_HEADS + h:B_HEADS + h + 1]
        li_r = glog_t[h:h + 1, :]
        b_r = bcum_t[B_HEADS + h:B_HEADS + h + 1, :]
        b_tot = b_r[:, L - 1:L]
        m_st = m_scr[h:h + 1, 0:1]
        state = state_scr[h]

        dmat = jnp.where(causal, b_c - b_r + li_r, -jnp.inf)
        m_inter = b_c + m_st
        m_j = jnp.maximum(m_inter, jnp.max(dmat, axis=-1, keepdims=True))
        sc = _dot_nt(qb, k_s.astype(BF16)) * jnp.exp(dmat - m_j)
        inter = jnp.exp(m_inter - m_j)
        ne = inter * _dot(qb, state.astype(BF16)) + _dot(sc.astype(BF16), v_ext)
        den = ne[:, B_HEAD_DIM:B_HEAD_DIM + 1]
        hj = ne[:, :B_HEAD_DIM] / jnp.maximum(jnp.abs(den), jnp.exp(-m_j))

        g_c = b_tot - b_c + li_c
        g_r = b_tot - b_r + li_r
        m_new = jnp.maximum(b_tot + m_st, jnp.max(g_r, axis=-1, keepdims=True))
        decay = jnp.exp(b_tot + m_st - m_new)
        kw = (k_s * jnp.exp(g_c - m_new)).astype(BF16)
        state_scr[h] = decay * state + _dot_tn(kw, v_ext)
        m_scr[h:h + 1, :] = jnp.broadcast_to(m_new, (1, LANES))

        mu = jnp.mean(hj, axis=-1, keepdims=True)
        var = jnp.mean(jnp.square(hj - mu), axis=-1, keepdims=True)
        hn = (hj - mu) * lax.rsqrt(var + EPS)
        out = (hn * gn_ref[:, sl] + skip_ref[:, sl] * xc[:, sl]) * jax.nn.sigmoid(og_ref[:, sl])
        y_ref[:, sl] = out.astype(BF16)


def _mlstm(xm, vm, og, conv_w, conv_b, wq, wk, wgt, bgt, gn_g, skip, batch):
    t = xm.shape[0]
    s = t // batch
    per_step = 2 if batch % 2 == 0 else 1
    seq3 = lambda a: a.reshape(batch, s, B_WIDTH)
    row = pl.BlockSpec((per_step, B_CHUNK, B_WIDTH), lambda b, c: (b, c, 0))
    y = pl.pallas_call(
        _mlstm_body,
        grid=(batch // per_step, s // B_CHUNK),
        in_specs=[row, row, row, _full(conv_w.shape), _full(conv_b.shape), _full(wq.shape), _full(wk.shape),
                  _full(wgt.shape), _full(bgt.shape), _full(gn_g.shape), _full(skip.shape)],
        out_specs=row,
        out_shape=jax.ShapeDtypeStruct((batch, s, B_WIDTH), BF16),
        scratch_shapes=[pltpu.VMEM((per_step, 8 + B_CHUNK, B_WIDTH), F32),
                        pltpu.VMEM((per_step, B_CHUNK, 3 * B_WIDTH), BF16),
                        pltpu.VMEM((per_step, B_HEADS, B_HEAD_DIM, 2 * LANES), F32),
                        pltpu.VMEM((per_step, 8, LANES), F32)],
        compiler_params=_params("parallel", "arbitrary"),
        name="mlstm",
    )(seq3(xm), seq3(vm), seq3(og), conv_w, conv_b, wq, wk, wgt, bgt, gn_g, skip)
    return y.reshape(t, B_WIDTH)


C_Z = C_Q_LORA + C_KV_LORA + IDX_HEADS * LANES + LANES + LANES


def _dsa_prep_val(x, g, w_in_ref, cq_g_ref, ckv_g_ref, w_uq_ref, w_ukt_ref,
                  qlat_ref, ckv_ref, qi_ref, ki_ref, wi_ref):
    h = _rms(x, g).astype(BF16)
    z = _dot(h, w_in_ref[...])
    o = 0
    cq = z[:, o:o + C_Q_LORA]; o += C_Q_LORA
    ckv = z[:, o:o + C_KV_LORA]; o += C_KV_LORA
    qi_ref[...] = z[:, o:o + IDX_HEADS * LANES].astype(BF16); o += IDX_HEADS * LANES
    ki_ref[...] = z[:, o:o + LANES].astype(BF16); o += LANES
    wi_ref[...] = z[:, o:o + LANES]
    ckv_ref[...] = _rms(ckv, ckv_g_ref[...]).astype(BF16)
    q = _dot(_rms(cq, cq_g_ref[...]).astype(BF16), w_uq_ref[...]).astype(BF16)
    scale = C_NOPE ** -0.5
    for hd in range(C_HEADS):
        ql = _dot(q[:, hd * C_NOPE:(hd + 1) * C_NOPE], w_ukt_ref[hd]) * scale
        qlat_ref[:, hd * C_KV_LORA:(hd + 1) * C_KV_LORA] = ql.astype(BF16)


def _pre_dsa_body(x_ref, g_ref, wg_ref, wu_ref, wd_ref, w_in_ref, cq_g_ref, ckv_g_ref, w_uq_ref, w_ukt_ref,
                  xo_ref, qlat_ref, ckv_ref, qi_ref, ki_ref, wi_ref):
    x = _ffn_val(x_ref[...], g_ref[0:1, :], g_ref[1:2, :], wg_ref, wu_ref, wd_ref)
    xo_ref[...] = x
    _dsa_prep_val(x, g_ref[2:3, :], w_in_ref, cq_g_ref, ckv_g_ref, w_uq_ref, w_ukt_ref,
                  qlat_ref, ckv_ref, qi_ref, ki_ref, wi_ref)


def _pre_dsa(x, g, wg, wu, wd, layer, w_in, cq_g, ckv_g, w_uq, w_ukt, *, tm):
    t, d = x.shape
    f = wg.shape[-1]
    once = dict(pipeline_mode=pl.Buffered(1))
    resident = lambda r, c: pl.BlockSpec((None, None, r, c), lambda i: (layer, 0, 0, 0), **once)
    const = lambda a: pl.BlockSpec(a.shape, lambda i: (0,) * a.ndim, **once)
    row = lambda w: pl.BlockSpec((tm, w), lambda i: (i, 0))
    return pl.pallas_call(
        _pre_dsa_body,
        grid=(t // tm,),
        in_specs=[row(d), _full(g.shape), resident(d, f), resident(d, f), resident(f, d),
                  const(w_in), const(cq_g), const(ckv_g), const(w_uq), const(w_ukt)],
        out_specs=[row(d), row(C_HEADS * C_KV_LORA), row(C_KV_LORA), row(IDX_HEADS * LANES), row(LANES), row(LANES)],
        out_shape=[jax.ShapeDtypeStruct((t, d), F32),
                   jax.ShapeDtypeStruct((t, C_HEADS * C_KV_LORA), BF16),
                   jax.ShapeDtypeStruct((t, C_KV_LORA), BF16),
                   jax.ShapeDtypeStruct((t, IDX_HEADS * LANES), BF16),
                   jax.ShapeDtypeStruct((t, LANES), BF16),
                   jax.ShapeDtypeStruct((t, LANES), F32)],
        compiler_params=pltpu.CompilerParams(dimension_semantics=("parallel",), vmem_limit_bytes=FUSED_VMEM_LIMIT),
        name="pre_dsa",
    )(x, g, wg, wu, wd, w_in, cq_g, ckv_g, w_uq, w_ukt)


INT_MAX = 2 ** 31 - 1
DSA_VMEM_LIMIT = 56 * 1024 * 1024


def _key_to_f32(key):
    return pltpu.bitcast(key ^ ((key >> 31) & 0x7FFFFFFF), F32)


def _dsa_attn_body(topk, seq, qi_ref, wi_ref, qlat_ref, ki_ref, ckv_ref, w_uv_ref, o_ref,
                   score_scr, key_scr, cnt_scr, bound_scr, m_scr, l_scr, acc_scr):
    Q = Q_BLOCK
    H = C_HEADS
    nb = seq // Q
    first = topk // Q
    col_iota = lax.broadcasted_iota(jnp.int32, (Q, Q), 1)
    ones = jnp.ones((Q, LANES), BF16)
    topk_f = float(topk)

    qa = 2 * Q
    qa_col = lax.broadcasted_iota(jnp.int32, (qa, qa), 1)
    qa_row = lax.broadcasted_iota(jnp.int32, (qa, qa), 0)

    def score_block(blk, _):
        r0 = pl.multiple_of(blk * qa, qa)
        qi = jnp.concatenate([qi_ref[pl.ds(r0, qa), h * LANES:(h + 1) * LANES] for h in range(IDX_HEADS)], axis=0)
        wi = wi_ref[pl.ds(r0, qa), :]
        wb = [jnp.broadcast_to(wi[:, h:h + 1], (qa, qa)) for h in range(IDX_HEADS)]

        def tile(c, _):
            k0 = pl.multiple_of(c * qa, qa)
            rel = jnp.maximum(_dot_nt(qi, ki_ref[pl.ds(k0, qa), :]), 0.0)
            score = rel[0:qa] * wb[0]
            for h in range(1, IDX_HEADS):
                score += rel[h * qa:(h + 1) * qa] * wb[h]
            score = jnp.where(k0 + qa_col <= r0 + qa_row, score, -jnp.inf)
            score_scr[pl.ds(r0, qa), pl.ds(k0, qa)] = score
            return 0

        lax.fori_loop(0, blk + 1, tile, 0)
        return 0

    lax.fori_loop(0, seq // qa, score_block, 0)

    if first > 0:
        key_scr[0:first * Q, :] = jnp.full((first * Q, LANES), KEY_LOWEST_FINITE, jnp.int32)
        cnt_scr[0:first * Q, :] = jnp.full((first * Q, LANES), topk_f, F32)
    for i in range(first, nb):
        key_scr[i * Q:(i + 1) * Q, :] = jnp.full((Q, LANES), INT_MIN, jnp.int32)
        cnt_scr[i * Q:(i + 1) * Q, :] = jnp.full((Q, LANES), float((i + 1) * Q), F32)

    def bit_step(b, _):
        bit = jnp.left_shift(jnp.int32(1), 31 - b)
        for i in range(first, nb):
            rows = slice(i * Q, (i + 1) * Q)
            key = key_scr[rows, :]
            cand = jnp.where(b == 0, jnp.zeros_like(key), key | bit)
            cand_f = _key_to_f32(cand)
            hits = jnp.zeros((Q, LANES), F32)
            for c in range(i + 1):
                hits += jnp.where(score_scr[rows, c * Q:(c + 1) * Q] >= cand_f, 1.0, 0.0)
            cnt = _dot(hits.astype(BF16), ones)
            ok = cnt >= topk_f
            key_scr[rows, :] = jnp.where(ok, cand, key)
            cnt_scr[rows, :] = jnp.where(ok, cnt, cnt_scr[rows, :])
        return 0

    lax.fori_loop(0, 32, bit_step, 0)

    bound_scr[...] = jnp.full(bound_scr.shape, INT_MAX, jnp.int32)

    idx_bits = (seq - 1).bit_length()

    def tie_block(i, _):
        r0 = pl.multiple_of(i * Q, Q)

        @pl.when(jnp.max(cnt_scr[pl.ds(r0, Q), :]) > topk_f)
        def _():
            thr = _key_to_f32(key_scr[pl.ds(r0, Q), :])

            def count(pred_fn):
                def body(c, hits):
                    k0 = pl.multiple_of(c * Q, Q)
                    return hits + jnp.where(pred_fn(score_scr[pl.ds(r0, Q), pl.ds(k0, Q)], k0), 1.0, 0.0)
                return _dot(lax.fori_loop(0, i + 1, body, jnp.zeros((Q, LANES), F32)).astype(BF16), ones)

            need = topk_f - count(lambda sc, k0: sc > thr)

            def pos_step(b, bound):
                cand = bound - jnp.right_shift(jnp.int32(1 << (idx_bits - 1)), b)
                cnt = count(lambda sc, k0: (sc == thr) & (k0 + col_iota <= cand))
                return jnp.where(cnt >= need, cand, bound)

            all_keys = jnp.full((Q, LANES), (1 << idx_bits) - 1, jnp.int32)
            bound_scr[pl.ds(r0, Q), :] = lax.fori_loop(0, idx_bits, pos_step, all_keys)
        return 0

    lax.fori_loop(first, nb, tie_block, 0)

    twice = lambda a: jnp.concatenate([a, a], axis=1)

    def attn_block(blk, _):
        r0 = pl.multiple_of(blk * qa, qa)
        qlat = jnp.concatenate([qlat_ref[pl.ds(r0, qa), h * C_KV_LORA:(h + 1) * C_KV_LORA] for h in range(H)], axis=0)
        thr = _key_to_f32(key_scr[pl.ds(r0, qa), :])
        bound = bound_scr[pl.ds(r0, qa), :]
        m_scr[...] = jnp.full(m_scr.shape, -jnp.inf, F32)
        l_scr[...] = jnp.zeros(l_scr.shape, F32)
        acc_scr[...] = jnp.zeros(acc_scr.shape, F32)

        def tile(k0, width):
            wide = lambda a: jnp.concatenate([a] * (width // LANES), axis=1)
            sc = score_scr[pl.ds(r0, qa), pl.ds(k0, width)]
            kidx = k0 + lax.broadcasted_iota(jnp.int32, (qa, width), 1)
            thr_w = wide(thr)
            sel = (sc > thr_w) | ((sc == thr_w) & (kidx <= wide(bound)))
            ckv = ckv_ref[pl.ds(k0, width), :]
            logits = _dot_nt(qlat, ckv)
            for h in range(H):
                rs = slice(h * qa, (h + 1) * qa)
                s = jnp.where(sel, logits[rs], -jnp.inf)
                m_old = m_scr[rs]
                m_new = jnp.maximum(m_old, jnp.max(s, axis=-1, keepdims=True))
                m_safe = jnp.where(m_new == -jnp.inf, 0.0, m_new)
                alpha = jnp.exp(m_old - m_safe)
                p = jnp.exp(s - wide(m_safe))
                l_scr[rs] = alpha * l_scr[rs] + jnp.sum(p, axis=-1, keepdims=True)
                acc_scr[rs] = twice(alpha) * acc_scr[rs] + _dot(p.astype(BF16), ckv)
                m_scr[rs] = m_new

        def tile_pair(cp, carry):
            tile(pl.multiple_of(cp * 2 * qa, 2 * qa), 2 * qa)
            return carry

        lax.fori_loop(0, (blk + 1) // 2, tile_pair, 0)

        @pl.when(blk % 2 == 0)
        def _():
            tile(pl.multiple_of(blk * qa, qa), qa)

        for h in range(H):
            rs = slice(h * qa, (h + 1) * qa)
            o_lat = (acc_scr[rs] / twice(l_scr[rs])).astype(BF16)
            o_ref[pl.ds(r0, qa), h * C_V:(h + 1) * C_V] = _dot(o_lat, w_uv_ref[h]).astype(BF16)
        return 0

    lax.fori_loop(0, seq // qa, attn_block, 0)


def _dsa_attn(qi, wi, qlat, ki, ckv, w_uv, batch):
    t = qi.shape[0]
    s = t // batch
    topk = min(TOPK_MAX, s // 4)
    assert topk % Q_BLOCK == 0 and s % (2 * Q_BLOCK) == 0 and s // Q_BLOCK <= 256
    seq = lambda w: pl.BlockSpec((s, w), lambda b: (b, 0), pipeline_mode=pl.Buffered(1))
    qa = 2 * Q_BLOCK
    return pl.pallas_call(
        functools.partial(_dsa_attn_body, topk, s),
        grid=(batch,),
        in_specs=[seq(IDX_HEADS * LANES), seq(LANES), seq(C_HEADS * C_KV_LORA), seq(LANES), seq(C_KV_LORA),
                  _full(w_uv.shape)],
        out_specs=pl.BlockSpec((s, C_HEADS * C_V), lambda b: (b, 0)),
        out_shape=jax.ShapeDtypeStruct((t, C_HEADS * C_V), BF16),
        scratch_shapes=[pltpu.VMEM((s, s), F32),
                        pltpu.VMEM((s, LANES), jnp.int32),
                        pltpu.VMEM((s, LANES), F32),
                        pltpu.VMEM((s, LANES), jnp.int32),
                        pltpu.VMEM((C_HEADS * qa, LANES), F32),
                        pltpu.VMEM((C_HEADS * qa, LANES), F32),
                        pltpu.VMEM((C_HEADS * qa, C_KV_LORA), F32)],
        compiler_params=pltpu.CompilerParams(dimension_semantics=("arbitrary",), vmem_limit_bytes=DSA_VMEM_LIMIT),
        name="dsa_attn",
    )(qi, wi, qlat, ki, ckv, w_uv)


def _mem_kv_body(mem_ref, g_ref, wk_ref, wv_ref, k_ref, v_ref):
    mn = _rms(mem_ref[...], g_ref[...]).astype(BF16)
    k_ref[...] = _dot(mn, wk_ref[...]).astype(BF16)
    v_ref[...] = _dot(mn, wv_ref[...]).astype(BF16)


def _mem_kv(mem, g, wk, wv, layer, n_mem):
    rows, d = mem.shape
    w = wk.shape[-1]
    wspec = pl.BlockSpec((None, d, w), lambda b: (layer, 0, 0))
    return pl.pallas_call(
        _mem_kv_body,
        grid=(rows // n_mem,),
        in_specs=[pl.BlockSpec((n_mem, d), lambda b: (b, 0)), _full((1, d)), wspec, wspec],
        out_specs=[pl.BlockSpec((n_mem, w), lambda b: (b, 0))] * 2,
        out_shape=[jax.ShapeDtypeStruct((rows, w), BF16)] * 2,
        compiler_params=_params("parallel"),
        name="mem_kv",
    )(mem, g, wk, wv)


def _xattn_val(x, gpre, gpost, wq_ref, wo_ref, k_ref, v_ref, att_scr):
    h = _rms(x, gpre).astype(BF16)
    q = (_dot(h, wq_ref[...]) * X_HEAD_DIM ** -0.5).astype(BF16)
    for hd in range(X_HEADS):
        sl = slice(hd * X_HEAD_DIM, (hd + 1) * X_HEAD_DIM)
        s = _dot_nt(q[:, sl], k_ref[:, sl])
        p = jnp.exp(s - jnp.max(s, axis=-1, keepdims=True))
        o = _dot(p.astype(BF16), v_ref[:, sl]) / jnp.sum(p, axis=-1, keepdims=True)
        att_scr[:, sl] = o.astype(BF16)
    return x + _rms(_dot(att_scr[...], wo_ref[...]), gpost)


def _post_body(n_in, x_ref, g_ref, *refs):
    a_refs, w_refs = refs[:n_in], refs[n_in:2 * n_in]
    wq_ref, wo_ref, k_ref, v_ref, wg_ref, wu_ref, wd_ref, o_ref, att_scr = refs[2 * n_in:]
    y = _dot(a_refs[0][...], w_refs[0][...])
    for a_ref, w_ref in zip(a_refs[1:], w_refs[1:]):
        y += _dot(a_ref[...], w_ref[...])
    x = x_ref[...] + _rms(y, g_ref[3:4, :])
    x = _xattn_val(x, g_ref[4:5, :], g_ref[5:6, :], wq_ref, wo_ref, k_ref, v_ref, att_scr)
    o_ref[...] = _ffn_val(x, g_ref[6:7, :], g_ref[7:8, :], wg_ref, wu_ref, wd_ref)


def _post_mixer(x, g, acts, weights, xa_wq, xa_wo, mk, mv, wg, wu, wd, layer, batch, *, tm):
    t, d = x.shape
    n_in = len(acts)
    w = xa_wq.shape[-1]
    f = wg.shape[-1]
    n_mem = mk.shape[0] // batch
    tiles_per_batch = t // batch // tm
    once = dict(pipeline_mode=pl.Buffered(1))
    resident = lambda r, c: pl.BlockSpec((None, None, r, c), lambda i: (layer, 1, 0, 0), **once)
    return pl.pallas_call(
        functools.partial(_post_body, n_in),
        grid=(t // tm,),
        in_specs=[pl.BlockSpec((tm, d), lambda i: (i, 0)), _full(g.shape)]
        + [pl.BlockSpec((tm, a.shape[1]), lambda i: (i, 0)) for a in acts]
        + [pl.BlockSpec(wt.shape, lambda i: (0, 0), **once) for wt in weights]
        + [pl.BlockSpec((None, d, w), lambda i: (layer, 0, 0), **once),
           pl.BlockSpec((None, w, d), lambda i: (layer, 0, 0), **once),
           pl.BlockSpec((n_mem, w), lambda i: (i // tiles_per_batch, 0)),
           pl.BlockSpec((n_mem, w), lambda i: (i // tiles_per_batch, 0)),
           resident(d, f), resident(d, f), resident(f, d)],
        out_specs=pl.BlockSpec((tm, d), lambda i: (i, 0)),
        out_shape=jax.ShapeDtypeStruct((t, d), F32),
        scratch_shapes=[pltpu.VMEM((tm, w), BF16)],
        compiler_params=pltpu.CompilerParams(dimension_semantics=("parallel",), vmem_limit_bytes=FUSED_VMEM_LIMIT),
        name="post_mixer",
    )(x, g, *acts, *weights, xa_wq, xa_wo, mk, mv, wg, wu, wd)


def _pad_cols(w, width):
    return jnp.pad(w, ((0, 0), (0, width - w.shape[1])))


def _prep_ab_weights(w_in, w_uq, w_ukv, w_gates, b_gates):
    o = A_Q_LORA + A_KV_LORA
    w_in_p = jnp.concatenate([w_in[:, :o], _pad_cols(w_in[:, o:o + A_ROPE], LANES), w_in[:, o + A_ROPE:]], axis=1)
    uq = w_uq.reshape(A_Q_LORA, A_HEADS, A_NOPE + A_ROPE)
    uq = jnp.pad(uq, ((0, 0), (0, 0), (0, A_HEAD_PAD - A_NOPE - A_ROPE))).reshape(A_Q_LORA, A_HEADS * A_HEAD_PAD)
    ukv = w_ukv.reshape(A_KV_LORA, A_HEADS, A_NOPE + A_V)
    ukv = jnp.concatenate([ukv[:, :, :A_NOPE].reshape(A_KV_LORA, -1), ukv[:, :, A_NOPE:].reshape(A_KV_LORA, -1)], axis=1)
    return (w_in_p.astype(BF16), uq.astype(BF16), ukv.astype(BF16),
            _pad_cols(w_gates, LANES).astype(BF16), _pad_cols(b_gates[None, :], LANES))


def _prep_c_weights(w_in, w_uk):
    o = C_Q_LORA + C_KV_LORA
    qi = w_in[:, o:o + IDX_HEADS * IDX_DIM].reshape(-1, IDX_HEADS, IDX_DIM)
    qi = jnp.pad(qi, ((0, 0), (0, 0), (0, LANES - IDX_DIM))).reshape(-1, IDX_HEADS * LANES)
    o2 = o + IDX_HEADS * IDX_DIM
    w_in_p = jnp.concatenate([w_in[:, :o], qi, _pad_cols(w_in[:, o2:o2 + IDX_DIM], LANES),
                              _pad_cols(w_in[:, o2 + IDX_DIM:], LANES)], axis=1)
    return w_in_p.astype(BF16), jnp.swapaxes(w_uk, 1, 2).astype(BF16)


def kernel(x, mem, positions, norm_g, mem_norm_g, ffn_w_gate, ffn_w_up, ffn_w_down, xa_wq, xa_wk, xa_wv, xa_wo, ab_w_in, ab_w_out, mla_cq_g, mla_ckv_g, mla_w_uq, mla_w_ukv, ml_conv_w, ml_conv_b, ml_wq, ml_wk, ml_w_gates, ml_b_gates, ml_gn_g, ml_skip, c_w_in, c_w_out, c_cq_g, c_ckv_g, c_w_uq, c_w_uk, c_w_uv):
    batch, seq, d = x.shape
    depth = norm_g.shape[0]
    n_mem = mem.shape[1]
    t = batch * seq
    tm = min(512, seq)

    xt = x.reshape(t, d)
    memt = mem.reshape(batch * n_mem, d)
    pos = positions.reshape(t, 1).astype(F32)
    wg, wu, wd = ffn_w_gate.astype(BF16), ffn_w_up.astype(BF16), ffn_w_down.astype(BF16)
    xa_wq_b, xa_wk_b, xa_wv_b, xa_wo_b = (w.astype(BF16) for w in (xa_wq, xa_wk, xa_wv, xa_wo))

    for l in range(depth):
        g = norm_g[l]
        mk, mv = _mem_kv(memt, mem_norm_g[l][None], xa_wk_b, xa_wv_b, l, n_mem)
        if l % 2 == 0:
            e = l // 2
            w_in_p, uq, ukv, wgt, bgt = _prep_ab_weights(ab_w_in[e], mla_w_uq[e], mla_w_ukv[e],
                                                         ml_w_gates[e], ml_b_gates[e])
            xt, q, k, v, xm, vm, og = _pre_ab(xt, pos, g, wg, wu, wd, l, w_in_p, mla_cq_g[e][None],
                                              mla_ckv_g[e][None], uq, ukv, tm=tm)
            ya = _mla_attn(q, k, v, batch, tq=min(256, seq))
            yb = _mlstm(xm, vm, og, ml_conv_w[e], ml_conv_b[e][None], ml_wq[e].astype(BF16), ml_wk[e].astype(BF16),
                        wgt, bgt, ml_gn_g[e][None], ml_skip[e][None], batch)
            w_out = ab_w_out[e].astype(BF16)
            acts, weights = [ya, yb], [w_out[:A_HEADS * A_V], w_out[A_HEADS * A_V:]]
        else:
            o = l // 2
            w_in_p, w_ukt = _prep_c_weights(c_w_in[o], c_w_uk[o])
            xt, qlat, ckv, qi, ki, wi = _pre_dsa(xt, g, wg, wu, wd, l, w_in_p, c_cq_g[o][None], c_ckv_g[o][None],
                                                 c_w_uq[o].astype(BF16), w_ukt, tm=tm)
            yc = _dsa_attn(qi, wi, qlat, ki, ckv, c_w_uv[o].astype(BF16), batch)
            acts, weights = [yc], [c_w_out[o].astype(BF16)]
        xt = _post_mixer(xt, g, acts, weights, xa_wq_b, xa_wo_b, mk, mv, wg, wu, wd, l, batch, tm=tm)
    return xt.reshape(batch, seq, d)
```

```python
import functools
import math

import jax
import jax.numpy as jnp
from jax import lax
from jax.experimental import pallas as pl
from jax.experimental.pallas import tpu as pltpu

F32 = jnp.float32
BF16 = jnp.bfloat16

EPS = 1e-6
ROPE_BASE = 10000.0
LANES = 128
VMEM_LIMIT = 48 * 1024 * 1024

A_HEADS, A_Q_LORA, A_KV_LORA, A_NOPE, A_ROPE, A_V = 4, 384, 256, 128, 64, 128
B_HEADS, B_HEAD_DIM, B_CONV, B_CHUNK = 4, 128, 4, 128
B_WIDTH = B_HEADS * B_HEAD_DIM
C_HEADS, C_Q_LORA, C_KV_LORA, C_NOPE, C_V = 8, 384, 256, 128, 128
IDX_HEADS, IDX_DIM, TOPK_MAX = 8, 64, 256
X_HEADS, X_HEAD_DIM = 4, 128
Q_BLOCK = 128

INT_MIN = -(2 ** 31)
KEY_LOWEST_FINITE = INT_MIN + 0x00800000


def _params(*sem):
    return pltpu.CompilerParams(dimension_semantics=sem, vmem_limit_bytes=VMEM_LIMIT)


def _rms(x, g):
    return x * lax.rsqrt(jnp.mean(x * x, axis=-1, keepdims=True) + EPS) * g


def _dot(a, b):
    return jnp.dot(a, b, preferred_element_type=F32)


def _dot_nt(a, b):
    return lax.dot_general(a, b, (((1,), (1,)), ((), ())), preferred_element_type=F32)


def _dot_tn(a, b):
    return lax.dot_general(a, b, (((0,), (0,)), ((), ())), preferred_element_type=F32)


def _full(shape):
    return pl.BlockSpec(shape, lambda *_: (0,) * len(shape))


def _stacked(a, idx, **kw):
    return pl.BlockSpec((None,) + a.shape[1:], lambda *_: (idx,) + (0,) * (a.ndim - 1), **kw)


FUSED_VMEM_LIMIT = 56 * 1024 * 1024
MXU_DIM = 256


def _ffn_chunks(f):
    if f % MXU_DIM:
        return [(0, f)]
    first = (f // MXU_DIM + 1) // 2 * MXU_DIM
    return [(0, first), (first, f)] if first < f else [(0, f)]


def _ffn_val(x, gpre, gpost, wg_ref, wu_ref, wd_ref):
    h = _rms(x, gpre).astype(BF16)
    y = None
    for lo, hi in _ffn_chunks(wg_ref.shape[1]):
        cols = slice(lo, hi)
        g = _dot(h, wg_ref[:, cols])
        u = _dot(h, wu_ref[:, cols])
        a = (g * jax.nn.sigmoid(g) * u).astype(BF16)
        part = _dot(a, wd_ref[cols, :])
        y = part if y is None else y + part
    return x + 0.5 * _rms(y, gpost)


def _rope_tables(pos):
    half = A_ROPE // 2
    lane = lax.broadcasted_iota(jnp.int32, (1, LANES), 1)
    j = (lane % half).astype(F32)
    inv = jnp.exp(j * (-math.log(ROPE_BASE) / half))
    ang = pos * inv
    sin = jnp.where(lane < A_ROPE, jnp.sin(ang), 0.0)
    return jnp.cos(ang), sin


def _rope_apply(r, cos, sin):
    half = A_ROPE // 2
    return r * cos + (pltpu.roll(r, half, axis=1) - pltpu.roll(r, LANES - half, axis=1)) * sin


AB_Z = A_Q_LORA + A_KV_LORA + LANES + 3 * B_WIDTH
A_HEAD_PAD = 2 * LANES


def _ab_prep_val(x, pos, g, w_in_ref, cq_g_ref, ckv_g_ref, w_uq_ref, w_ukv_ref,
                 q_ref, k_ref, v_ref, xm_ref, vm_ref, og_ref):
    h = _rms(x, g).astype(BF16)
    z = _dot(h, w_in_ref[...])
    o = 0
    cq = z[:, o:o + A_Q_LORA]; o += A_Q_LORA
    ckv = z[:, o:o + A_KV_LORA]; o += A_KV_LORA
    kr = z[:, o:o + LANES]; o += LANES
    xm_ref[...] = z[:, o:o + B_WIDTH]; o += B_WIDTH
    vm_ref[...] = z[:, o:o + B_WIDTH]; o += B_WIDTH
    og_ref[...] = z[:, o:o + B_WIDTH]

    cos, sin = _rope_tables(pos)
    scale = (A_NOPE + A_ROPE) ** -0.5
    q = _dot(_rms(cq, cq_g_ref[...]).astype(BF16), w_uq_ref[...]) * scale
    kv = _dot(_rms(ckv, ckv_g_ref[...]).astype(BF16), w_ukv_ref[...])
    k_rope = _rope_apply(kr, cos, sin).astype(BF16)
    for hd in range(A_HEADS):
        base = hd * A_HEAD_PAD
        q_ref[:, base:base + A_NOPE] = q[:, base:base + A_NOPE].astype(BF16)
        q_ref[:, base + A_NOPE:base + A_HEAD_PAD] = _rope_apply(
            q[:, base + A_NOPE:base + A_HEAD_PAD], cos, sin).astype(BF16)
        k_ref[:, base:base + A_NOPE] = kv[:, hd * A_NOPE:(hd + 1) * A_NOPE].astype(BF16)
        k_ref[:, base + A_NOPE:base + A_HEAD_PAD] = k_rope
    v_ref[...] = kv[:, A_HEADS * A_NOPE:].astype(BF16)


def _pre_ab_body(x_ref, pos_ref, g_ref, wg_ref, wu_ref, wd_ref, w_in_ref, cq_g_ref, ckv_g_ref, w_uq_ref,
                 w_ukv_ref, xo_ref, q_ref, k_ref, v_ref, xm_ref, vm_ref, og_ref):
    x = _ffn_val(x_ref[...], g_ref[0:1, :], g_ref[1:2, :], wg_ref, wu_ref, wd_ref)
    xo_ref[...] = x
    _ab_prep_val(x, pos_ref[...], g_ref[2:3, :], w_in_ref, cq_g_ref, ckv_g_ref, w_uq_ref, w_ukv_ref,
                 q_ref, k_ref, v_ref, xm_ref, vm_ref, og_ref)


def _pre_ab(x, pos, g, wg, wu, wd, layer, e, w_in, cq_g, ckv_g, w_uq, w_ukv, *, tm):
    t, d = x.shape
    f = wg.shape[-1]
    once = dict(pipeline_mode=pl.Buffered(1))
    resident = lambda r, c: pl.BlockSpec((None, None, r, c), lambda i: (layer, 0, 0, 0), **once)
    const = lambda a: _stacked(a, e, **once)
    row = lambda w: pl.BlockSpec((tm, w), lambda i: (i, 0))
    return pl.pallas_call(
        _pre_ab_body,
        grid=(t // tm,),
        in_specs=[row(d), row(1), _stacked(g, layer), resident(d, f), resident(d, f), resident(f, d),
                  const(w_in), const(cq_g), const(ckv_g), const(w_uq), const(w_ukv)],
        out_specs=[row(d), row(A_HEADS * A_HEAD_PAD), row(A_HEADS * A_HEAD_PAD), row(A_HEADS * A_V),
                   row(B_WIDTH), row(B_WIDTH), row(B_WIDTH)],
        out_shape=[jax.ShapeDtypeStruct((t, d), F32),
                   jax.ShapeDtypeStruct((t, A_HEADS * A_HEAD_PAD), BF16),
                   jax.ShapeDtypeStruct((t, A_HEADS * A_HEAD_PAD), BF16),
                   jax.ShapeDtypeStruct((t, A_HEADS * A_V), BF16),
                   jax.ShapeDtypeStruct((t, B_WIDTH), F32),
                   jax.ShapeDtypeStruct((t, B_WIDTH), F32),
                   jax.ShapeDtypeStruct((t, B_WIDTH), F32)],
        compiler_params=pltpu.CompilerParams(dimension_semantics=("parallel",), vmem_limit_bytes=FUSED_VMEM_LIMIT),
        name="pre_ab",
    )(x, pos, g, wg, wu, wd, w_in, cq_g, ckv_g, w_uq, w_ukv)


def _mla_attn_body(tq, q_ref, k_ref, v_ref, o_ref, m_scr, l_scr, acc_scr):
    i = pl.program_id(1)
    heads = range(A_HEADS)
    m_scr[...] = jnp.full(m_scr.shape, -jnp.inf, F32)
    l_scr[...] = jnp.zeros(l_scr.shape, F32)
    acc_scr[...] = jnp.zeros(acc_scr.shape, F32)

    def chunk(start, width, masked):
        widen = lambda a: jnp.concatenate([a] * (width // LANES), axis=1)
        for h in heads:
            s = _dot_nt(q_ref[:, h * A_HEAD_PAD:(h + 1) * A_HEAD_PAD],
                        k_ref[pl.ds(start, width), h * A_HEAD_PAD:(h + 1) * A_HEAD_PAD])
            if masked:
                row = lax.broadcasted_iota(jnp.int32, (tq, width), 0)
                col = lax.broadcasted_iota(jnp.int32, (tq, width), 1)
                s = jnp.where(col <= row, s, -jnp.inf)
            m_old = m_scr[h]
            m_new = jnp.maximum(m_old, jnp.max(s, axis=-1, keepdims=True))
            alpha = jnp.exp(m_old - m_new)
            p = jnp.exp(s - widen(m_new))
            l_scr[h] = alpha * l_scr[h] + jnp.sum(p, axis=-1, keepdims=True)
            acc_scr[h] = alpha * acc_scr[h] + _dot(p.astype(BF16), v_ref[pl.ds(start, width), h * A_V:(h + 1) * A_V])
            m_scr[h] = m_new

    def chunk_pair(jp, carry):
        chunk(pl.multiple_of(jp * 2 * tq, 2 * tq), 2 * tq, False)
        return carry

    lax.fori_loop(0, i // 2, chunk_pair, 0)

    @pl.when(i % 2 == 1)
    def _():
        chunk(pl.multiple_of((i - 1) * tq, tq), tq, False)

    chunk(pl.multiple_of(i * tq, tq), tq, True)
    for h in heads:
        o_ref[:, h * A_V:(h + 1) * A_V] = (acc_scr[h] / l_scr[h]).astype(BF16)


def _mla_attn(q, k, v, batch, *, tq):
    t = q.shape[0]
    s = t // batch
    nq = s // tq
    return pl.pallas_call(
        functools.partial(_mla_attn_body, tq),
        grid=(batch, nq),
        in_specs=[pl.BlockSpec((tq, A_HEADS * A_HEAD_PAD), lambda b, i: (b * nq + i, 0)),
                  pl.BlockSpec((s, A_HEADS * A_HEAD_PAD), lambda b, i: (b, 0)),
                  pl.BlockSpec((s, A_HEADS * A_V), lambda b, i: (b, 0))],
        out_specs=pl.BlockSpec((tq, A_HEADS * A_V), lambda b, i: (b * nq + i, 0)),
        out_shape=jax.ShapeDtypeStruct((t, A_HEADS * A_V), BF16),
        scratch_shapes=[pltpu.VMEM((A_HEADS, tq, LANES), F32), pltpu.VMEM((A_HEADS, tq, LANES), F32),
                        pltpu.VMEM((A_HEADS, tq, A_V), F32)],
        compiler_params=_params("parallel", "arbitrary"),
        name="mla_attn",
    )(q, k, v)


def _log_sigmoid(x):
    return jnp.minimum(x, 0.0) - jnp.log(1.0 + jnp.exp(-jnp.abs(x)))


def _mlstm_body(xm_ref, vm_ref, og_ref, conv_w_ref, conv_b_ref, wq_ref, wk_ref, wgt_ref, bgt_ref,
                gn_ref, skip_ref, y_ref, *scratch):
    c = pl.program_id(1)
    L = B_CHUNK
    tail = 8
    per_elem = [scratch[4 * e:4 * e + 4] for e in range(xm_ref.shape[0])]

    @pl.when(c == 0)
    def _():
        for xext_scr, _, state_scr, m_scr in per_elem:
            xext_scr[0:tail, :] = jnp.zeros((tail, B_WIDTH), F32)
            state_scr[...] = jnp.zeros_like(state_scr)
            m_scr[...] = jnp.zeros_like(m_scr)

    lane = lax.broadcasted_iota(jnp.int32, (L, LANES), 1)
    row = lax.broadcasted_iota(jnp.int32, (L, LANES), 0)
    is_f = (lane >= B_HEADS) & (lane < 2 * B_HEADS)
    causal = lax.broadcasted_iota(jnp.int32, (L, L), 1) <= lax.broadcasted_iota(jnp.int32, (L, L), 0)
    ones_col = jnp.where(lane == 0, 1.0, 0.0).astype(BF16)
    for e, (xext_scr, gin_scr, state_scr, m_scr) in enumerate(per_elem):
        _mlstm_chunk(xm_ref.at[e], vm_ref.at[e], og_ref.at[e], conv_w_ref, conv_b_ref, wq_ref, wk_ref, wgt_ref,
                     bgt_ref, gn_ref, skip_ref, y_ref.at[e], xext_scr, gin_scr, state_scr, m_scr,
                     row, is_f, causal, ones_col)


def _mlstm_chunk(xm_ref, vm_ref, og_ref, conv_w_ref, conv_b_ref, wq_ref, wk_ref, wgt_ref, bgt_ref,
                 gn_ref, skip_ref, y_ref, xext_scr, gin_scr, state_scr, m_scr, row, is_f, causal, ones_col):
    L = B_CHUNK
    tail = 8
    xm = xm_ref[...]
    xext_scr[tail:tail + L, :] = xm
    conv = conv_b_ref[...] + conv_w_ref[B_CONV - 1:B_CONV, :] * xm
    for tap in range(B_CONV - 1):
        shift = B_CONV - 1 - tap
        conv += conv_w_ref[tap:tap + 1, :] * xext_scr[tail - shift:tail - shift + L, :]
    xext_scr[0:tail, :] = xm[L - tail:, :]
    xc = conv * jax.nn.sigmoid(conv)
    xcb = xc.astype(BF16)

    vm = vm_ref[...]
    qs, ks = [], []
    for h in range(B_HEADS):
        sl = slice(h * B_HEAD_DIM, (h + 1) * B_HEAD_DIM)
        qh = _dot(xcb[:, sl], wq_ref[h])
        kh = _dot(xcb[:, sl], wk_ref[h])
        qs.append(qh)
        ks.append(kh)
        gin_scr[:, h * B_HEAD_DIM:(h + 1) * B_HEAD_DIM] = qh.astype(BF16)
        gin_scr[:, B_WIDTH + h * B_HEAD_DIM:B_WIDTH + (h + 1) * B_HEAD_DIM] = kh.astype(BF16)
    gin_scr[:, 2 * B_WIDTH:] = vm.astype(BF16)
    gates = _dot(gin_scr[...], wgt_ref[...]) + bgt_ref[...]

    glog = jnp.where(is_f, _log_sigmoid(gates), gates)
    bcum = jnp.where(is_f, glog, 0.0)
    step = 1
    while step < L:
        bcum = bcum + jnp.where(row >= step, pltpu.roll(bcum, step, axis=0), 0.0)
        step *= 2
    glog_t = glog.T
    bcum_t = bcum.T

    kscale = B_HEAD_DIM ** -0.5
    for h in range(B_HEADS):
        sl = slice(h * B_HEAD_DIM, (h + 1) * B_HEAD_DIM)
        qb = qs[h].astype(BF16)
        k_s = ks[h] * kscale
        v_ext = jnp.concatenate([vm[:, sl].astype(BF16), ones_col], axis=1)
        li_c = glog[:, h:h + 1]
        b_c = bcum[:, B_HEADS + h:B_HEADS + h + 1]
        li_r = glog_t[h:h + 1, :]
        b_r = bcum_t[B_HEADS + h:B_HEADS + h + 1, :]
        b_tot = b_r[:, L - 1:L]
        m_st = m_scr[h:h + 1, 0:1]
        state = state_scr[h]

        dmat = jnp.where(causal, b_c - b_r + li_r, -jnp.inf)
        m_inter = b_c + m_st
        m_j = jnp.maximum(m_inter, jnp.max(dmat, axis=-1, keepdims=True))
        sc = _dot_nt(qb, k_s.astype(BF16)) * jnp.exp(dmat - m_j)
        inter = jnp.exp(m_inter - m_j)
        ne = inter * _dot(qb, state.astype(BF16)) + _dot(sc.astype(BF16), v_ext)
        den = ne[:, B_HEAD_DIM:B_HEAD_DIM + 1]
        hj = ne[:, :B_HEAD_DIM] / jnp.maximum(jnp.abs(den), jnp.exp(-m_j))

        g_c = b_tot - b_c + li_c
        g_r = b_tot - b_r + li_r
        m_new = jnp.maximum(b_tot + m_st, jnp.max(g_r, axis=-1, keepdims=True))
        decay = jnp.exp(b_tot + m_st - m_new)
        kw = (k_s * jnp.exp(g_c - m_new)).astype(BF16)
        state_scr[h] = decay * state + _dot_tn(kw, v_ext)
        m_scr[h:h + 1, :] = jnp.broadcast_to(m_new, (1, LANES))

        mu = jnp.mean(hj, axis=-1, keepdims=True)
        var = jnp.mean(jnp.square(hj - mu), axis=-1, keepdims=True)
        hn = (hj - mu) * lax.rsqrt(var + EPS)
        out = (hn * gn_ref[:, sl] + skip_ref[:, sl] * xc[:, sl]) * jax.nn.sigmoid(og_ref[:, sl])
        y_ref[:, sl] = out.astype(BF16)


def _mlstm(xm, vm, og, e, conv_w, conv_b, wq, wk, wgt, bgt, gn_g, skip, batch):
    t = xm.shape[0]
    s = t // batch
    per_step = 2 if batch % 2 == 0 else 1
    seq3 = lambda a: a.reshape(batch, s, B_WIDTH)
    row = pl.BlockSpec((per_step, B_CHUNK, B_WIDTH), lambda b, c: (b, c, 0))
    weights = (conv_w, conv_b, wq, wk, wgt, bgt, gn_g, skip)
    y = pl.pallas_call(
        _mlstm_body,
        grid=(batch // per_step, s // B_CHUNK),
        in_specs=[row, row, row] + [_stacked(a, e) for a in weights],
        out_specs=row,
        out_shape=jax.ShapeDtypeStruct((batch, s, B_WIDTH), BF16),
        scratch_shapes=[pltpu.VMEM((8 + B_CHUNK, B_WIDTH), F32),
                        pltpu.VMEM((B_CHUNK, 3 * B_WIDTH), BF16),
                        pltpu.VMEM((B_HEADS, B_HEAD_DIM, 2 * LANES), F32),
                        pltpu.VMEM((8, LANES), F32)] * per_step,
        compiler_params=_params("parallel", "arbitrary"),
        name="mlstm",
    )(seq3(xm), seq3(vm), seq3(og), *weights)
    return y.reshape(t, B_WIDTH)


C_Z = C_Q_LORA + C_KV_LORA + IDX_HEADS * LANES + LANES + LANES


def _dsa_prep_val(x, g, w_in_ref, cq_g_ref, ckv_g_ref, w_uq_ref, w_ukt_ref,
                  qlat_ref, ckv_ref, qi_ref, ki_ref, wi_ref):
    h = _rms(x, g).astype(BF16)
    z = _dot(h, w_in_ref[...])
    o = 0
    cq = z[:, o:o + C_Q_LORA]; o += C_Q_LORA
    ckv = z[:, o:o + C_KV_LORA]; o += C_KV_LORA
    qi_ref[...] = z[:, o:o + IDX_HEADS * LANES].astype(BF16); o += IDX_HEADS * LANES
    ki_ref[...] = z[:, o:o + LANES].astype(BF16); o += LANES
    wi_ref[...] = z[:, o:o + LANES]
    ckv_ref[...] = _rms(ckv, ckv_g_ref[...]).astype(BF16)
    q = _dot(_rms(cq, cq_g_ref[...]).astype(BF16), w_uq_ref[...]).astype(BF16)
    scale = C_NOPE ** -0.5
    for hd in range(C_HEADS):
        ql = _dot(q[:, hd * C_NOPE:(hd + 1) * C_NOPE], w_ukt_ref[hd]) * scale
        qlat_ref[:, hd * C_KV_LORA:(hd + 1) * C_KV_LORA] = ql.astype(BF16)


def _pre_dsa_body(x_ref, g_ref, wg_ref, wu_ref, wd_ref, w_in_ref, cq_g_ref, ckv_g_ref, w_uq_ref, w_ukt_ref,
                  xo_ref, qlat_ref, ckv_ref, qi_ref, ki_ref, wi_ref):
    x = _ffn_val(x_ref[...], g_ref[0:1, :], g_ref[1:2, :], wg_ref, wu_ref, wd_ref)
    xo_ref[...] = x
    _dsa_prep_val(x, g_ref[2:3, :], w_in_ref, cq_g_ref, ckv_g_ref, w_uq_ref, w_ukt_ref,
                  qlat_ref, ckv_ref, qi_ref, ki_ref, wi_ref)


def _pre_dsa(x, g, wg, wu, wd, layer, o, w_in, cq_g, ckv_g, w_uq, w_ukt, *, tm):
    t, d = x.shape
    f = wg.shape[-1]
    once = dict(pipeline_mode=pl.Buffered(1))
    resident = lambda r, c: pl.BlockSpec((None, None, r, c), lambda i: (layer, 0, 0, 0), **once)
    const = lambda a: _stacked(a, o, **once)
    row = lambda w: pl.BlockSpec((tm, w), lambda i: (i, 0))
    return pl.pallas_call(
        _pre_dsa_body,
        grid=(t // tm,),
        in_specs=[row(d), _stacked(g, layer), resident(d, f), resident(d, f), resident(f, d),
                  const(w_in), const(cq_g), const(ckv_g), const(w_uq), const(w_ukt)],
        out_specs=[row(d), row(C_HEADS * C_KV_LORA), row(C_KV_LORA), row(IDX_HEADS * LANES), row(LANES), row(LANES)],
        out_shape=[jax.ShapeDtypeStruct((t, d), F32),
                   jax.ShapeDtypeStruct((t, C_HEADS * C_KV_LORA), BF16),
                   jax.ShapeDtypeStruct((t, C_KV_LORA), BF16),
                   jax.ShapeDtypeStruct((t, IDX_HEADS * LANES), BF16),
                   jax.ShapeDtypeStruct((t, LANES), BF16),
                   jax.ShapeDtypeStruct((t, LANES), F32)],
        compiler_params=pltpu.CompilerParams(dimension_semantics=("parallel",), vmem_limit_bytes=FUSED_VMEM_LIMIT),
        name="pre_dsa",
    )(x, g, wg, wu, wd, w_in, cq_g, ckv_g, w_uq, w_ukt)


INT_MAX = 2 ** 31 - 1
DSA_VMEM_LIMIT = 56 * 1024 * 1024


def _key_to_f32(key):
    return pltpu.bitcast(key ^ ((key >> 31) & 0x7FFFFFFF), F32)


def _dsa_attn_body(topk, seq, qi_ref, wi_ref, qlat_ref, ki_ref, ckv_ref, w_uv_ref, o_ref,
                   score_scr, key_scr, cnt_scr, bound_scr, m_scr, l_scr, acc_scr):
    Q = Q_BLOCK
    H = C_HEADS
    nb = seq // Q
    first = topk // Q
    col_iota = lax.broadcasted_iota(jnp.int32, (Q, Q), 1)
    ones = jnp.ones((Q, LANES), BF16)
    topk_f = float(topk)

    qa = 2 * Q
    qa_col = lax.broadcasted_iota(jnp.int32, (qa, qa), 1)
    qa_row = lax.broadcasted_iota(jnp.int32, (qa, qa), 0)

    def score_block(blk, _):
        r0 = pl.multiple_of(blk * qa, qa)
        qi = jnp.concatenate([qi_ref[pl.ds(r0, qa), h * LANES:(h + 1) * LANES] for h in range(IDX_HEADS)], axis=0)
        wi = wi_ref[pl.ds(r0, qa), :]
        wb = [jnp.broadcast_to(wi[:, h:h + 1], (qa, qa)) for h in range(IDX_HEADS)]

        def tile(c, _):
            k0 = pl.multiple_of(c * qa, qa)
            rel = jnp.maximum(_dot_nt(qi, ki_ref[pl.ds(k0, qa), :]), 0.0)
            score = rel[0:qa] * wb[0]
            for h in range(1, IDX_HEADS):
                score += rel[h * qa:(h + 1) * qa] * wb[h]
            score = jnp.where(k0 + qa_col <= r0 + qa_row, score, -jnp.inf)
            score_scr[pl.ds(r0, qa), pl.ds(k0, qa)] = score
            return 0

        lax.fori_loop(0, blk + 1, tile, 0)
        return 0

    lax.fori_loop(0, seq // qa, score_block, 0)

    if first > 0:
        key_scr[0:first * Q, :] = jnp.full((first * Q, LANES), KEY_LOWEST_FINITE, jnp.int32)
        cnt_scr[0:first * Q, :] = jnp.full((first * Q, LANES), topk_f, F32)
    for i in range(first, nb):
        key_scr[i * Q:(i + 1) * Q, :] = jnp.full((Q, LANES), INT_MIN, jnp.int32)
        cnt_scr[i * Q:(i + 1) * Q, :] = jnp.full((Q, LANES), float((i + 1) * Q), F32)

    def bit_step(b, _):
        bit = jnp.left_shift(jnp.int32(1), 31 - b)
        for i in range(first, nb):
            rows = slice(i * Q, (i + 1) * Q)
            key = key_scr[rows, :]
            cand = jnp.where(b == 0, jnp.zeros_like(key), key | bit)
            cand_f = _key_to_f32(cand)
            hits = jnp.zeros((Q, LANES), F32)
            for c in range(i + 1):
                hits += jnp.where(score_scr[rows, c * Q:(c + 1) * Q] >= cand_f, 1.0, 0.0)
            cnt = _dot(hits.astype(BF16), ones)
            ok = cnt >= topk_f
            key_scr[rows, :] = jnp.where(ok, cand, key)
            cnt_scr[rows, :] = jnp.where(ok, cnt, cnt_scr[rows, :])
        return 0

    lax.fori_loop(0, 32, bit_step, 0)

    bound_scr[...] = jnp.full(bound_scr.shape, INT_MAX, jnp.int32)

    idx_bits = (seq - 1).bit_length()

    def tie_block(i, _):
        r0 = pl.multiple_of(i * Q, Q)

        @pl.when(jnp.max(cnt_scr[pl.ds(r0, Q), :]) > topk_f)
        def _():
            thr = _key_to_f32(key_scr[pl.ds(r0, Q), :])

            def count(pred_fn):
                def body(c, hits):
                    k0 = pl.multiple_of(c * Q, Q)
                    return hits + jnp.where(pred_fn(score_scr[pl.ds(r0, Q), pl.ds(k0, Q)], k0), 1.0, 0.0)
                return _dot(lax.fori_loop(0, i + 1, body, jnp.zeros((Q, LANES), F32)).astype(BF16), ones)

            need = topk_f - count(lambda sc, k0: sc > thr)

            def pos_step(b, bound):
                cand = bound - jnp.right_shift(jnp.int32(1 << (idx_bits - 1)), b)
                cnt = count(lambda sc, k0: (sc == thr) & (k0 + col_iota <= cand))
                return jnp.where(cnt >= need, cand, bound)

            all_keys = jnp.full((Q, LANES), (1 << idx_bits) - 1, jnp.int32)
            bound_scr[pl.ds(r0, Q), :] = lax.fori_loop(0, idx_bits, pos_step, all_keys)
        return 0

    lax.fori_loop(first, nb, tie_block, 0)

    twice = lambda a: jnp.concatenate([a, a], axis=1)

    def attn_block(blk, _):
        r0 = pl.multiple_of(blk * qa, qa)
        qlat = jnp.concatenate([qlat_ref[pl.ds(r0, qa), h * C_KV_LORA:(h + 1) * C_KV_LORA] for h in range(H)], axis=0)
        thr = _key_to_f32(key_scr[pl.ds(r0, qa), :])
        bound = bound_scr[pl.ds(r0, qa), :]
        m_scr[...] = jnp.full(m_scr.shape, -jnp.inf, F32)
        l_scr[...] = jnp.zeros(l_scr.shape, F32)
        acc_scr[...] = jnp.zeros(acc_scr.shape, F32)

        def tile(k0, width):
            wide = lambda a: jnp.concatenate([a] * (width // LANES), axis=1)
            sc = score_scr[pl.ds(r0, qa), pl.ds(k0, width)]
            kidx = k0 + lax.broadcasted_iota(jnp.int32, (qa, width), 1)
            thr_w = wide(thr)
            sel = (sc > thr_w) | ((sc == thr_w) & (kidx <= wide(bound)))
            ckv = ckv_ref[pl.ds(k0, width), :]
            logits = _dot_nt(qlat, ckv)
            for h in range(H):
                rs = slice(h * qa, (h + 1) * qa)
                s = jnp.where(sel, logits[rs], -jnp.inf)
                m_old = m_scr[rs]
                m_new = jnp.maximum(m_old, jnp.max(s, axis=-1, keepdims=True))
                m_safe = jnp.where(m_new == -jnp.inf, 0.0, m_new)
                alpha = jnp.exp(m_old - m_safe)
                p = jnp.exp(s - wide(m_safe))
                l_scr[rs] = alpha * l_scr[rs] + jnp.sum(p, axis=-1, keepdims=True)
                acc_scr[rs] = twice(alpha) * acc_scr[rs] + _dot(p.astype(BF16), ckv)
                m_scr[rs] = m_new

        def tile_pair(cp, carry):
            tile(pl.multiple_of(cp * 2 * qa, 2 * qa), 2 * qa)
            return carry

        lax.fori_loop(0, (blk + 1) // 2, tile_pair, 0)

        @pl.when(blk % 2 == 0)
        def _():
            tile(pl.multiple_of(blk * qa, qa), qa)

        for h in range(H):
            rs = slice(h * qa, (h + 1) * qa)
            o_lat = (acc_scr[rs] / twice(l_scr[rs])).astype(BF16)
            o_ref[pl.ds(r0, qa), h * C_V:(h + 1) * C_V] = _dot(o_lat, w_uv_ref[h]).astype(BF16)
        return 0

    lax.fori_loop(0, seq // qa, attn_block, 0)


def _dsa_attn(qi, wi, qlat, ki, ckv, o, w_uv, batch):
    t = qi.shape[0]
    s = t // batch
    topk = min(TOPK_MAX, s // 4)
    assert topk % Q_BLOCK == 0 and s % (2 * Q_BLOCK) == 0 and s // Q_BLOCK <= 256
    seq = lambda w: pl.BlockSpec((s, w), lambda b: (b, 0), pipeline_mode=pl.Buffered(1))
    qa = 2 * Q_BLOCK
    return pl.pallas_call(
        functools.partial(_dsa_attn_body, topk, s),
        grid=(batch,),
        in_specs=[seq(IDX_HEADS * LANES), seq(LANES), seq(C_HEADS * C_KV_LORA), seq(LANES), seq(C_KV_LORA),
                  _stacked(w_uv, o)],
        out_specs=pl.BlockSpec((s, C_HEADS * C_V), lambda b: (b, 0)),
        out_shape=jax.ShapeDtypeStruct((t, C_HEADS * C_V), BF16),
        scratch_shapes=[pltpu.VMEM((s, s), F32),
                        pltpu.VMEM((s, LANES), jnp.int32),
                        pltpu.VMEM((s, LANES), F32),
                        pltpu.VMEM((s, LANES), jnp.int32),
                        pltpu.VMEM((C_HEADS * qa, LANES), F32),
                        pltpu.VMEM((C_HEADS * qa, LANES), F32),
                        pltpu.VMEM((C_HEADS * qa, C_KV_LORA), F32)],
        compiler_params=pltpu.CompilerParams(dimension_semantics=("arbitrary",), vmem_limit_bytes=DSA_VMEM_LIMIT),
        name="dsa_attn",
    )(qi, wi, qlat, ki, ckv, w_uv)


def _mem_kv_body(mem_ref, g_ref, wk_ref, wv_ref, k_ref, v_ref):
    mn = _rms(mem_ref[...], g_ref[...]).astype(BF16)
    k_ref[...] = _dot(mn, wk_ref[...]).astype(BF16)
    v_ref[...] = _dot(mn, wv_ref[...]).astype(BF16)


def _mem_kv(mem, g, wk, wv, n_mem):
    rows, d = mem.shape
    depth, _, w = wk.shape
    wspec = pl.BlockSpec((None, d, w), lambda l, b: (l, 0, 0))
    out = pl.BlockSpec((None, n_mem, w), lambda l, b: (l, b, 0))
    return pl.pallas_call(
        _mem_kv_body,
        grid=(depth, rows // n_mem),
        in_specs=[pl.BlockSpec((n_mem, d), lambda l, b: (b, 0)), pl.BlockSpec((None, 1, d), lambda l, b: (l, 0, 0)),
                  wspec, wspec],
        out_specs=[out, out],
        out_shape=[jax.ShapeDtypeStruct((depth, rows, w), BF16)] * 2,
        compiler_params=_params("parallel", "parallel"),
        name="mem_kv",
    )(mem, g, wk, wv)


def _xattn_val(x, gpre, gpost, wq_ref, wo_ref, k_ref, v_ref, att_scr):
    h = _rms(x, gpre).astype(BF16)
    q = (_dot(h, wq_ref[...]) * X_HEAD_DIM ** -0.5).astype(BF16)
    for hd in range(X_HEADS):
        sl = slice(hd * X_HEAD_DIM, (hd + 1) * X_HEAD_DIM)
        s = _dot_nt(q[:, sl], k_ref[:, sl])
        p = jnp.exp(s - jnp.max(s, axis=-1, keepdims=True))
        o = _dot(p.astype(BF16), v_ref[:, sl]) / jnp.sum(p, axis=-1, keepdims=True)
        att_scr[:, sl] = o.astype(BF16)
    return x + _rms(_dot(att_scr[...], wo_ref[...]), gpost)


def _post_body(n_in, x_ref, g_ref, *refs):
    a_refs, w_refs = refs[:n_in], refs[n_in:2 * n_in]
    wq_ref, wo_ref, k_ref, v_ref, wg_ref, wu_ref, wd_ref, o_ref, att_scr = refs[2 * n_in:]
    y = _dot(a_refs[0][...], w_refs[0][...])
    for a_ref, w_ref in zip(a_refs[1:], w_refs[1:]):
        y += _dot(a_ref[...], w_ref[...])
    x = x_ref[...] + _rms(y, g_ref[3:4, :])
    x = _xattn_val(x, g_ref[4:5, :], g_ref[5:6, :], wq_ref, wo_ref, k_ref, v_ref, att_scr)
    o_ref[...] = _ffn_val(x, g_ref[6:7, :], g_ref[7:8, :], wg_ref, wu_ref, wd_ref)


def _post_mixer(x, g, proj, xa_wq, xa_wo, mk, mv, wg, wu, wd, layer, batch, *, tm):
    t, d = x.shape
    acts = [a for a, _, _ in proj]
    w = xa_wq.shape[-1]
    f = wg.shape[-1]
    n_mem = mk.shape[1] // batch
    tiles_per_batch = t // batch // tm
    once = dict(pipeline_mode=pl.Buffered(1))
    resident = lambda r, c: pl.BlockSpec((None, None, r, c), lambda i: (layer, 1, 0, 0), **once)
    mem_spec = pl.BlockSpec((None, n_mem, w), lambda i: (layer, i // tiles_per_batch, 0))
    return pl.pallas_call(
        functools.partial(_post_body, len(proj)),
        grid=(t // tm,),
        in_specs=[pl.BlockSpec((tm, d), lambda i: (i, 0)), _stacked(g, layer)]
        + [pl.BlockSpec((tm, a.shape[1]), lambda i: (i, 0)) for a in acts]
        + [spec for _, _, spec in proj]
        + [_stacked(xa_wq, layer, **once), _stacked(xa_wo, layer, **once), mem_spec, mem_spec,
           resident(d, f), resident(d, f), resident(f, d)],
        out_specs=pl.BlockSpec((tm, d), lambda i: (i, 0)),
        out_shape=jax.ShapeDtypeStruct((t, d), F32),
        scratch_shapes=[pltpu.VMEM((tm, w), BF16)],
        compiler_params=pltpu.CompilerParams(dimension_semantics=("parallel",), vmem_limit_bytes=FUSED_VMEM_LIMIT),
        name="post_mixer",
    )(x, g, *acts, *[wt for _, wt, _ in proj], xa_wq, xa_wo, mk, mv, wg, wu, wd)


def _pad_last(a, width):
    return jnp.pad(a, [(0, 0)] * (a.ndim - 1) + [(0, width - a.shape[-1])])


def _prep_ab_weights(w_in, w_uq, w_ukv, w_gates, b_gates):
    n = w_in.shape[0]
    o = A_Q_LORA + A_KV_LORA
    w_in_p = jnp.concatenate([w_in[..., :o], _pad_last(w_in[..., o:o + A_ROPE], LANES), w_in[..., o + A_ROPE:]], axis=-1)
    uq = _pad_last(w_uq.reshape(n, A_Q_LORA, A_HEADS, A_NOPE + A_ROPE), A_HEAD_PAD)
    uq = uq.reshape(n, A_Q_LORA, A_HEADS * A_HEAD_PAD)
    ukv = w_ukv.reshape(n, A_KV_LORA, A_HEADS, A_NOPE + A_V)
    ukv = jnp.concatenate([ukv[..., :A_NOPE].reshape(n, A_KV_LORA, -1), ukv[..., A_NOPE:].reshape(n, A_KV_LORA, -1)],
                          axis=-1)
    return (w_in_p.astype(BF16), uq.astype(BF16), ukv.astype(BF16),
            _pad_last(w_gates, LANES).astype(BF16), _pad_last(b_gates[:, None, :], LANES))


def _prep_c_weights(w_in, w_uk):
    n = w_in.shape[0]
    o = C_Q_LORA + C_KV_LORA
    qi = _pad_last(w_in[..., o:o + IDX_HEADS * IDX_DIM].reshape(n, -1, IDX_HEADS, IDX_DIM), LANES)
    qi = qi.reshape(n, -1, IDX_HEADS * LANES)
    o2 = o + IDX_HEADS * IDX_DIM
    w_in_p = jnp.concatenate([w_in[..., :o], qi, _pad_last(w_in[..., o2:o2 + IDX_DIM], LANES),
                              _pad_last(w_in[..., o2 + IDX_DIM:], LANES)], axis=-1)
    return w_in_p.astype(BF16), jnp.swapaxes(w_uk, 2, 3).astype(BF16)


def kernel(x, mem, positions, norm_g, mem_norm_g, ffn_w_gate, ffn_w_up, ffn_w_down, xa_wq, xa_wk, xa_wv, xa_wo, ab_w_in, ab_w_out, mla_cq_g, mla_ckv_g, mla_w_uq, mla_w_ukv, ml_conv_w, ml_conv_b, ml_wq, ml_wk, ml_w_gates, ml_b_gates, ml_gn_g, ml_skip, c_w_in, c_w_out, c_cq_g, c_ckv_g, c_w_uq, c_w_uk, c_w_uv):
    batch, seq, d = x.shape
    depth = norm_g.shape[0]
    n_mem = mem.shape[1]
    t = batch * seq
    tm = min(512, seq)

    xt = x.reshape(t, d)
    memt = mem.reshape(batch * n_mem, d)
    pos = positions.reshape(t, 1).astype(F32)
    bf = lambda *ws: [w.astype(BF16) for w in ws]
    row3 = lambda a: a[:, None, :]
    once = dict(pipeline_mode=pl.Buffered(1))
    wg, wu, wd, xa_wq_b, xa_wk_b, xa_wv_b, xa_wo_b = bf(ffn_w_gate, ffn_w_up, ffn_w_down, xa_wq, xa_wk, xa_wv, xa_wo)
    ab_in, ab_uq, ab_ukv, ml_wgt, ml_bgt = _prep_ab_weights(ab_w_in, mla_w_uq, mla_w_ukv, ml_w_gates, ml_b_gates)
    c_in, c_ukt = _prep_c_weights(c_w_in, c_w_uk)
    ab_out, c_out, ml_wq_b, ml_wk_b, c_uq, c_uv = bf(ab_w_out, c_w_out, ml_wq, ml_wk, c_w_uq, c_w_uv)
    mk, mv = _mem_kv(memt, row3(mem_norm_g), xa_wk_b, xa_wv_b, n_mem)

    for l in range(depth):
        if l % 2 == 0:
            e = l // 2
            xt, q, k, v, xm, vm, og = _pre_ab(xt, pos, norm_g, wg, wu, wd, l, e, ab_in, row3(mla_cq_g), row3(mla_ckv_g),
                                              ab_uq, ab_ukv, tm=tm)
            ya = _mla_attn(q, k, v, batch, tq=min(256, seq))
            yb = _mlstm(xm, vm, og, e, ml_conv_w, row3(ml_conv_b), ml_wq_b, ml_wk_b, ml_wgt, ml_bgt,
                        row3(ml_gn_g), row3(ml_skip), batch)
            half = lambda part: pl.BlockSpec((None, A_HEADS * A_V, d), lambda i: (e, part, 0), **once)
            proj = [(ya, ab_out, half(0)), (yb, ab_out, half(1))]
        else:
            o = l // 2
            xt, qlat, ckv, qi, ki, wi = _pre_dsa(xt, norm_g, wg, wu, wd, l, o, c_in, row3(c_cq_g), row3(c_ckv_g),
                                                 c_uq, c_ukt, tm=tm)
            yc = _dsa_attn(qi, wi, qlat, ki, ckv, o, c_uv, batch)
            proj = [(yc, c_out, _stacked(c_out, o, **once))]
        xt = _post_mixer(xt, norm_g, proj, xa_wq_b, xa_wo_b, mk, mv, wg, wu, wd, l, batch, tm=tm)
    return xt.reshape(batch, seq, d)
```

```python
import functools
import math

import jax
import jax.numpy as jnp
from jax import lax
from jax.experimental import pallas as pl
from jax.experimental.pallas import tpu as pltpu

F32 = jnp.float32
BF16 = jnp.bfloat16

EPS = 1e-6
ROPE_BASE = 10000.0
LANES = 128
VMEM_LIMIT = 48 * 1024 * 1024

A_HEADS, A_Q_LORA, A_KV_LORA, A_NOPE, A_ROPE, A_V = 4, 384, 256, 128, 64, 128
B_HEADS, B_HEAD_DIM, B_CONV, B_CHUNK = 4, 128, 4, 128
B_WIDTH = B_HEADS * B_HEAD_DIM
C_HEADS, C_Q_LORA, C_KV_LORA, C_NOPE, C_V = 8, 384, 256, 128, 128
IDX_HEADS, IDX_DIM, TOPK_MAX = 8, 64, 256
X_HEADS, X_HEAD_DIM = 4, 128
Q_BLOCK = 128

LOG2E = math.log2(math.e)
INT_MIN = -(2 ** 31)
KEY_LOWEST_FINITE = INT_MIN + 0x00800000


def _params(*sem):
    return pltpu.CompilerParams(dimension_semantics=sem, vmem_limit_bytes=VMEM_LIMIT)


def _rms(x, g):
    return x * lax.rsqrt(jnp.mean(x * x, axis=-1, keepdims=True) + EPS) * g


def _dot(a, b):
    return jnp.dot(a, b, preferred_element_type=F32)


def _dot_nt(a, b):
    return lax.dot_general(a, b, (((1,), (1,)), ((), ())), preferred_element_type=F32)


def _dot_tn(a, b):
    return lax.dot_general(a, b, (((0,), (0,)), ((), ())), preferred_element_type=F32)


def _full(shape):
    return pl.BlockSpec(shape, lambda *_: (0,) * len(shape))


def _stacked(a, idx, **kw):
    return pl.BlockSpec((None,) + a.shape[1:], lambda *_: (idx,) + (0,) * (a.ndim - 1), **kw)


FUSED_VMEM_LIMIT = 56 * 1024 * 1024
MXU_DIM = 256


def _ffn_chunks(f):
    if f % MXU_DIM:
        return [(0, f)]
    first = (f // MXU_DIM + 1) // 2 * MXU_DIM
    return [(0, first), (first, f)] if first < f else [(0, f)]


def _ffn_val(x, gpre, gpost, wg_ref, wu_ref, wd_ref):
    h = _rms(x, gpre).astype(BF16)
    y = None
    for lo, hi in _ffn_chunks(wg_ref.shape[1]):
        cols = slice(lo, hi)
        g = _dot(h, wg_ref[:, cols])
        u = _dot(h, wu_ref[:, cols])
        a = (g * jax.nn.sigmoid(g) * u).astype(BF16)
        part = _dot(a, wd_ref[cols, :])
        y = part if y is None else y + part
    return x + 0.5 * _rms(y, gpost)


def _rope_tables(pos):
    half = A_ROPE // 2
    lane = lax.broadcasted_iota(jnp.int32, (1, LANES), 1)
    j = (lane % half).astype(F32)
    inv = jnp.exp(j * (-math.log(ROPE_BASE) / half))
    ang = pos * inv
    sin = jnp.where(lane < A_ROPE, jnp.sin(ang), 0.0)
    return jnp.cos(ang), sin


def _rope_apply(r, cos, sin):
    half = A_ROPE // 2
    return r * cos + (pltpu.roll(r, half, axis=1) - pltpu.roll(r, LANES - half, axis=1)) * sin


AB_Z = A_Q_LORA + A_KV_LORA + LANES + 3 * B_WIDTH
A_HEAD_PAD = 2 * LANES


def _ab_prep_val(x, pos, g, w_in_ref, cq_g_ref, ckv_g_ref, w_uq_ref, w_ukv_ref,
                 q_ref, k_ref, v_ref, xm_ref, vm_ref, og_ref):
    h = _rms(x, g).astype(BF16)
    z = _dot(h, w_in_ref[...])
    o = 0
    cq = z[:, o:o + A_Q_LORA]; o += A_Q_LORA
    ckv = z[:, o:o + A_KV_LORA]; o += A_KV_LORA
    kr = z[:, o:o + LANES]; o += LANES
    xm_ref[...] = z[:, o:o + B_WIDTH]; o += B_WIDTH
    vm_ref[...] = z[:, o:o + B_WIDTH]; o += B_WIDTH
    og_ref[...] = z[:, o:o + B_WIDTH]

    cos, sin = _rope_tables(pos)
    scale = (A_NOPE + A_ROPE) ** -0.5 * LOG2E
    q = _dot(_rms(cq, cq_g_ref[...]).astype(BF16), w_uq_ref[...]) * scale
    kv = _dot(_rms(ckv, ckv_g_ref[...]).astype(BF16), w_ukv_ref[...])
    k_rope = _rope_apply(kr, cos, sin).astype(BF16)
    for hd in range(A_HEADS):
        base = hd * A_HEAD_PAD
        q_ref[:, base:base + A_NOPE] = q[:, base:base + A_NOPE].astype(BF16)
        q_ref[:, base + A_NOPE:base + A_HEAD_PAD] = _rope_apply(
            q[:, base + A_NOPE:base + A_HEAD_PAD], cos, sin).astype(BF16)
        k_ref[:, base:base + A_NOPE] = kv[:, hd * A_NOPE:(hd + 1) * A_NOPE].astype(BF16)
        k_ref[:, base + A_NOPE:base + A_HEAD_PAD] = k_rope
    v_ref[...] = kv[:, A_HEADS * A_NOPE:].astype(BF16)


def _pre_ab_body(x_ref, pos_ref, g_ref, wg_ref, wu_ref, wd_ref, w_in_ref, cq_g_ref, ckv_g_ref, w_uq_ref,
                 w_ukv_ref, xo_ref, q_ref, k_ref, v_ref, xm_ref, vm_ref, og_ref):
    x = _ffn_val(x_ref[...], g_ref[0:1, :], g_ref[1:2, :], wg_ref, wu_ref, wd_ref)
    xo_ref[...] = x
    _ab_prep_val(x, pos_ref[...], g_ref[2:3, :], w_in_ref, cq_g_ref, ckv_g_ref, w_uq_ref, w_ukv_ref,
                 q_ref, k_ref, v_ref, xm_ref, vm_ref, og_ref)


def _pre_ab(x, pos, g, wg, wu, wd, layer, e, w_in, cq_g, ckv_g, w_uq, w_ukv, *, tm):
    t, d = x.shape
    f = wg.shape[-1]
    once = dict(pipeline_mode=pl.Buffered(1))
    resident = lambda r, c: pl.BlockSpec((None, None, r, c), lambda i: (layer, 0, 0, 0), **once)
    const = lambda a: _stacked(a, e, **once)
    row = lambda w: pl.BlockSpec((tm, w), lambda i: (i, 0))
    return pl.pallas_call(
        _pre_ab_body,
        grid=(t // tm,),
        in_specs=[row(d), row(1), _stacked(g, layer), resident(d, f), resident(d, f), resident(f, d),
                  const(w_in), const(cq_g), const(ckv_g), const(w_uq), const(w_ukv)],
        out_specs=[row(d), row(A_HEADS * A_HEAD_PAD), row(A_HEADS * A_HEAD_PAD), row(A_HEADS * A_V),
                   row(B_WIDTH), row(B_WIDTH), row(B_WIDTH)],
        out_shape=[jax.ShapeDtypeStruct((t, d), F32),
                   jax.ShapeDtypeStruct((t, A_HEADS * A_HEAD_PAD), BF16),
                   jax.ShapeDtypeStruct((t, A_HEADS * A_HEAD_PAD), BF16),
                   jax.ShapeDtypeStruct((t, A_HEADS * A_V), BF16),
                   jax.ShapeDtypeStruct((t, B_WIDTH), F32),
                   jax.ShapeDtypeStruct((t, B_WIDTH), F32),
                   jax.ShapeDtypeStruct((t, B_WIDTH), F32)],
        compiler_params=pltpu.CompilerParams(dimension_semantics=("parallel",), vmem_limit_bytes=FUSED_VMEM_LIMIT),
        name="pre_ab",
    )(x, pos, g, wg, wu, wd, w_in, cq_g, ckv_g, w_uq, w_ukv)


def _mla_attn_body(tq, q_ref, k_ref, v_ref, o_ref, m_scr, l_scr, acc_scr):
    i = pl.program_id(1)
    heads = range(A_HEADS)
    m_scr[...] = jnp.full(m_scr.shape, -jnp.inf, F32)
    l_scr[...] = jnp.zeros(l_scr.shape, F32)
    acc_scr[...] = jnp.zeros(acc_scr.shape, F32)

    def chunk(start, width, masked):
        widen = lambda a: jnp.concatenate([a] * (width // LANES), axis=1)
        for h in heads:
            s = _dot_nt(q_ref[:, h * A_HEAD_PAD:(h + 1) * A_HEAD_PAD],
                        k_ref[pl.ds(start, width), h * A_HEAD_PAD:(h + 1) * A_HEAD_PAD])
            if masked:
                row = lax.broadcasted_iota(jnp.int32, (tq, width), 0)
                col = lax.broadcasted_iota(jnp.int32, (tq, width), 1)
                s = jnp.where(col <= row, s, -jnp.inf)
            m_old = m_scr[h]
            m_new = jnp.maximum(m_old, jnp.max(s, axis=-1, keepdims=True))
            alpha = jnp.exp2(m_old - m_new)
            p = jnp.exp2(s - widen(m_new))
            l_scr[h] = alpha * l_scr[h] + jnp.sum(p, axis=-1, keepdims=True)
            acc_scr[h] = alpha * acc_scr[h] + _dot(p.astype(BF16), v_ref[pl.ds(start, width), h * A_V:(h + 1) * A_V])
            m_scr[h] = m_new

    def chunk_pair(jp, carry):
        chunk(pl.multiple_of(jp * 2 * tq, 2 * tq), 2 * tq, False)
        return carry

    lax.fori_loop(0, i // 2, chunk_pair, 0)

    @pl.when(i % 2 == 1)
    def _():
        chunk(pl.multiple_of((i - 1) * tq, tq), tq, False)

    chunk(pl.multiple_of(i * tq, tq), tq, True)
    for h in heads:
        o_ref[:, h * A_V:(h + 1) * A_V] = (acc_scr[h] / l_scr[h]).astype(BF16)


def _mla_attn(q, k, v, batch, *, tq):
    t = q.shape[0]
    s = t // batch
    nq = s // tq
    return pl.pallas_call(
        functools.partial(_mla_attn_body, tq),
        grid=(batch, nq),
        in_specs=[pl.BlockSpec((tq, A_HEADS * A_HEAD_PAD), lambda b, i: (b * nq + i, 0)),
                  pl.BlockSpec((s, A_HEADS * A_HEAD_PAD), lambda b, i: (b, 0)),
                  pl.BlockSpec((s, A_HEADS * A_V), lambda b, i: (b, 0))],
        out_specs=pl.BlockSpec((tq, A_HEADS * A_V), lambda b, i: (b * nq + i, 0)),
        out_shape=jax.ShapeDtypeStruct((t, A_HEADS * A_V), BF16),
        scratch_shapes=[pltpu.VMEM((A_HEADS, tq, LANES), F32), pltpu.VMEM((A_HEADS, tq, LANES), F32),
                        pltpu.VMEM((A_HEADS, tq, A_V), F32)],
        compiler_params=_params("parallel", "arbitrary"),
        name="mla_attn",
    )(q, k, v)


def _log_sigmoid(x):
    return jnp.minimum(x, 0.0) - jnp.log(1.0 + jnp.exp(-jnp.abs(x)))


def _mlstm_body(xm_ref, vm_ref, og_ref, conv_w_ref, conv_b_ref, wq_ref, wk_ref, wgt_ref, bgt_ref,
                gn_ref, skip_ref, y_ref, *scratch):
    c = pl.program_id(1)
    L = B_CHUNK
    tail = 8
    per_elem = [scratch[4 * e:4 * e + 4] for e in range(xm_ref.shape[0])]

    @pl.when(c == 0)
    def _():
        for xext_scr, _, state_scr, m_scr in per_elem:
            xext_scr[0:tail, :] = jnp.zeros((tail, B_WIDTH), F32)
            state_scr[...] = jnp.zeros_like(state_scr)
            m_scr[...] = jnp.zeros_like(m_scr)

    lane = lax.broadcasted_iota(jnp.int32, (L, LANES), 1)
    row = lax.broadcasted_iota(jnp.int32, (L, LANES), 0)
    is_f = (lane >= B_HEADS) & (lane < 2 * B_HEADS)
    causal = lax.broadcasted_iota(jnp.int32, (L, L), 1) <= lax.broadcasted_iota(jnp.int32, (L, L), 0)
    ones_col = jnp.where(lane == 0, 1.0, 0.0).astype(BF16)
    for e, (xext_scr, gin_scr, state_scr, m_scr) in enumerate(per_elem):
        _mlstm_chunk(xm_ref.at[e], vm_ref.at[e], og_ref.at[e], conv_w_ref, conv_b_ref, wq_ref, wk_ref, wgt_ref,
                     bgt_ref, gn_ref, skip_ref, y_ref.at[e], xext_scr, gin_scr, state_scr, m_scr,
                     row, is_f, causal, ones_col)


def _mlstm_chunk(xm_ref, vm_ref, og_ref, conv_w_ref, conv_b_ref, wq_ref, wk_ref, wgt_ref, bgt_ref,
                 gn_ref, skip_ref, y_ref, xext_scr, gin_scr, state_scr, m_scr, row, is_f, causal, ones_col):
    L = B_CHUNK
    tail = 8
    xm = xm_ref[...]
    xext_scr[tail:tail + L, :] = xm
    conv = conv_b_ref[...] + conv_w_ref[B_CONV - 1:B_CONV, :] * xm
    for tap in range(B_CONV - 1):
        shift = B_CONV - 1 - tap
        conv += conv_w_ref[tap:tap + 1, :] * xext_scr[tail - shift:tail - shift + L, :]
    xext_scr[0:tail, :] = xm[L - tail:, :]
    xc = conv * jax.nn.sigmoid(conv)
    xcb = xc.astype(BF16)

    vm = vm_ref[...]
    qs, ks = [], []
    for h in range(B_HEADS):
        sl = slice(h * B_HEAD_DIM, (h + 1) * B_HEAD_DIM)
        qh = _dot(xcb[:, sl], wq_ref[h])
        kh = _dot(xcb[:, sl], wk_ref[h])
        qs.append(qh)
        ks.append(kh)
        gin_scr[:, h * B_HEAD_DIM:(h + 1) * B_HEAD_DIM] = qh.astype(BF16)
        gin_scr[:, B_WIDTH + h * B_HEAD_DIM:B_WIDTH + (h + 1) * B_HEAD_DIM] = kh.astype(BF16)
    gin_scr[:, 2 * B_WIDTH:] = vm.astype(BF16)
    gates = _dot(gin_scr[...], wgt_ref[...]) + bgt_ref[...]

    glog = jnp.where(is_f, _log_sigmoid(gates), gates)
    bcum = jnp.where(is_f, glog, 0.0)
    step = 1
    while step < L:
        bcum = bcum + jnp.where(row >= step, pltpu.roll(bcum, step, axis=0), 0.0)
        step *= 2
    glog_t = glog.T
    bcum_t = bcum.T

    kscale = B_HEAD_DIM ** -0.5
    for h in range(B_HEADS):
        sl = slice(h * B_HEAD_DIM, (h + 1) * B_HEAD_DIM)
        qb = qs[h].astype(BF16)
        k_s = ks[h] * kscale
        v_ext = jnp.concatenate([vm[:, sl].astype(BF16), ones_col], axis=1)
        li_c = glog[:, h:h + 1]
        b_c = bcum[:, B_HEADS + h:B_HEADS + h + 1]
        li_r = glog_t[h:h + 1, :]
        b_r = bcum_t[B_HEADS + h:B_HEADS + h + 1, :]
        b_tot = b_r[:, L - 1:L]
        m_st = m_scr[h:h + 1, 0:1]
        state = state_scr[h]

        dmat = jnp.where(causal, b_c - b_r + li_r, -jnp.inf)
        m_inter = b_c + m_st
        m_j = jnp.maximum(m_inter, jnp.max(dmat, axis=-1, keepdims=True))
        sc = _dot_nt(qb, k_s.astype(BF16)) * jnp.exp(dmat - m_j)
        inter = jnp.exp(m_inter - m_j)
        ne = inter * _dot(qb, state.astype(BF16)) + _dot(sc.astype(BF16), v_ext)
        den = ne[:, B_HEAD_DIM:B_HEAD_DIM + 1]
        hj = ne[:, :B_HEAD_DIM] / jnp.maximum(jnp.abs(den), jnp.exp(-m_j))

        g_c = b_tot - b_c + li_c
        g_r = b_tot - b_r + li_r
        m_new = jnp.maximum(b_tot + m_st, jnp.max(g_r, axis=-1, keepdims=True))
        decay = jnp.exp(b_tot + m_st - m_new)
        kw = (k_s * jnp.exp(g_c - m_new)).astype(BF16)
        state_scr[h] = decay * state + _dot_tn(kw, v_ext)
        m_scr[h:h + 1, :] = jnp.broadcast_to(m_new, (1, LANES))

        mu = jnp.mean(hj, axis=-1, keepdims=True)
        var = jnp.mean(jnp.square(hj - mu), axis=-1, keepdims=True)
        hn = (hj - mu) * lax.rsqrt(var + EPS)
        out = (hn * gn_ref[:, sl] + skip_ref[:, sl] * xc[:, sl]) * jax.nn.sigmoid(og_ref[:, sl])
        y_ref[:, sl] = out.astype(BF16)


def _mlstm(xm, vm, og, e, conv_w, conv_b, wq, wk, wgt, bgt, gn_g, skip, batch):
    t = xm.shape[0]
    s = t // batch
    per_step = 2 if batch % 2 == 0 else 1
    seq3 = lambda a: a.reshape(batch, s, B_WIDTH)
    row = pl.BlockSpec((per_step, B_CHUNK, B_WIDTH), lambda b, c: (b, c, 0))
    weights = (conv_w, conv_b, wq, wk, wgt, bgt, gn_g, skip)
    y = pl.pallas_call(
        _mlstm_body,
        grid=(batch // per_step, s // B_CHUNK),
        in_specs=[row, row, row] + [_stacked(a, e) for a in weights],
        out_specs=row,
        out_shape=jax.ShapeDtypeStruct((batch, s, B_WIDTH), BF16),
        scratch_shapes=[pltpu.VMEM((8 + B_CHUNK, B_WIDTH), F32),
                        pltpu.VMEM((B_CHUNK, 3 * B_WIDTH), BF16),
                        pltpu.VMEM((B_HEADS, B_HEAD_DIM, 2 * LANES), F32),
                        pltpu.VMEM((8, LANES), F32)] * per_step,
        compiler_params=_params("parallel", "arbitrary"),
        name="mlstm",
    )(seq3(xm), seq3(vm), seq3(og), *weights)
    return y.reshape(t, B_WIDTH)


C_Z = C_Q_LORA + C_KV_LORA + IDX_HEADS * LANES + LANES + LANES


def _dsa_prep_val(x, g, w_in_ref, cq_g_ref, ckv_g_ref, w_uq_ref, w_ukt_ref,
                  qlat_ref, ckv_ref, qi_ref, ki_ref, wi_ref):
    h = _rms(x, g).astype(BF16)
    z = _dot(h, w_in_ref[...])
    o = 0
    cq = z[:, o:o + C_Q_LORA]; o += C_Q_LORA
    ckv = z[:, o:o + C_KV_LORA]; o += C_KV_LORA
    qi_ref[...] = z[:, o:o + IDX_HEADS * LANES].astype(BF16); o += IDX_HEADS * LANES
    ki_ref[...] = z[:, o:o + LANES].astype(BF16); o += LANES
    wi_ref[...] = z[:, o:o + LANES]
    ckv_ref[...] = _rms(ckv, ckv_g_ref[...]).astype(BF16)
    q = _dot(_rms(cq, cq_g_ref[...]).astype(BF16), w_uq_ref[...]).astype(BF16)
    scale = C_NOPE ** -0.5 * LOG2E
    for hd in range(C_HEADS):
        ql = _dot(q[:, hd * C_NOPE:(hd + 1) * C_NOPE], w_ukt_ref[hd]) * scale
        qlat_ref[:, hd * C_KV_LORA:(hd + 1) * C_KV_LORA] = ql.astype(BF16)


def _pre_dsa_body(x_ref, g_ref, wg_ref, wu_ref, wd_ref, w_in_ref, cq_g_ref, ckv_g_ref, w_uq_ref, w_ukt_ref,
                  xo_ref, qlat_ref, ckv_ref, qi_ref, ki_ref, wi_ref):
    x = _ffn_val(x_ref[...], g_ref[0:1, :], g_ref[1:2, :], wg_ref, wu_ref, wd_ref)
    xo_ref[...] = x
    _dsa_prep_val(x, g_ref[2:3, :], w_in_ref, cq_g_ref, ckv_g_ref, w_uq_ref, w_ukt_ref,
                  qlat_ref, ckv_ref, qi_ref, ki_ref, wi_ref)


def _pre_dsa(x, g, wg, wu, wd, layer, o, w_in, cq_g, ckv_g, w_uq, w_ukt, *, tm):
    t, d = x.shape
    f = wg.shape[-1]
    once = dict(pipeline_mode=pl.Buffered(1))
    resident = lambda r, c: pl.BlockSpec((None, None, r, c), lambda i: (layer, 0, 0, 0), **once)
    const = lambda a: _stacked(a, o, **once)
    row = lambda w: pl.BlockSpec((tm, w), lambda i: (i, 0))
    return pl.pallas_call(
        _pre_dsa_body,
        grid=(t // tm,),
        in_specs=[row(d), _stacked(g, layer), resident(d, f), resident(d, f), resident(f, d),
                  const(w_in), const(cq_g), const(ckv_g), const(w_uq), const(w_ukt)],
        out_specs=[row(d), row(C_HEADS * C_KV_LORA), row(C_KV_LORA), row(IDX_HEADS * LANES), row(LANES), row(LANES)],
        out_shape=[jax.ShapeDtypeStruct((t, d), F32),
                   jax.ShapeDtypeStruct((t, C_HEADS * C_KV_LORA), BF16),
                   jax.ShapeDtypeStruct((t, C_KV_LORA), BF16),
                   jax.ShapeDtypeStruct((t, IDX_HEADS * LANES), BF16),
                   jax.ShapeDtypeStruct((t, LANES), BF16),
                   jax.ShapeDtypeStruct((t, LANES), F32)],
        compiler_params=pltpu.CompilerParams(dimension_semantics=("parallel",), vmem_limit_bytes=FUSED_VMEM_LIMIT),
        name="pre_dsa",
    )(x, g, wg, wu, wd, w_in, cq_g, ckv_g, w_uq, w_ukt)


INT_MAX = 2 ** 31 - 1
DSA_VMEM_LIMIT = 56 * 1024 * 1024


def _key_to_f32(key):
    return pltpu.bitcast(key ^ ((key >> 31) & 0x7FFFFFFF), F32)


def _dsa_attn_body(topk, seq, qi_ref, wi_ref, qlat_ref, ki_ref, ckv_ref, w_uv_ref, o_ref,
                   score_scr, key_scr, cnt_scr, bound_scr, m_scr, l_scr, acc_scr):
    Q = Q_BLOCK
    H = C_HEADS
    nb = seq // Q
    first = topk // Q
    col_iota = lax.broadcasted_iota(jnp.int32, (Q, Q), 1)
    ones = jnp.ones((Q, LANES), BF16)
    topk_f = float(topk)

    qa = 2 * Q
    qa_col = lax.broadcasted_iota(jnp.int32, (qa, qa), 1)
    qa_row = lax.broadcasted_iota(jnp.int32, (qa, qa), 0)

    def score_block(blk, _):
        r0 = pl.multiple_of(blk * qa, qa)
        qi = jnp.concatenate([qi_ref[pl.ds(r0, qa), h * LANES:(h + 1) * LANES] for h in range(IDX_HEADS)], axis=0)
        wi = wi_ref[pl.ds(r0, qa), :]
        wb = [jnp.broadcast_to(wi[:, h:h + 1], (qa, qa)) for h in range(IDX_HEADS)]

        def tile(c, _):
            k0 = pl.multiple_of(c * qa, qa)
            rel = jnp.maximum(_dot_nt(qi, ki_ref[pl.ds(k0, qa), :]), 0.0)
            score = rel[0:qa] * wb[0]
            for h in range(1, IDX_HEADS):
                score += rel[h * qa:(h + 1) * qa] * wb[h]
            score = jnp.where(k0 + qa_col <= r0 + qa_row, score, -jnp.inf)
            score_scr[pl.ds(r0, qa), pl.ds(k0, qa)] = score
            return 0

        lax.fori_loop(0, blk + 1, tile, 0)
        return 0

    lax.fori_loop(0, seq // qa, score_block, 0)

    if first > 0:
        key_scr[0:first * Q, :] = jnp.full((first * Q, LANES), KEY_LOWEST_FINITE, jnp.int32)
        cnt_scr[0:first * Q, :] = jnp.full((first * Q, LANES), topk_f, F32)
    for i in range(first, nb):
        key_scr[i * Q:(i + 1) * Q, :] = jnp.full((Q, LANES), INT_MIN, jnp.int32)
        cnt_scr[i * Q:(i + 1) * Q, :] = jnp.full((Q, LANES), float((i + 1) * Q), F32)

    def bit_step(b, _):
        bit = jnp.left_shift(jnp.int32(1), 31 - b)
        for i in range(first, nb):
            rows = slice(i * Q, (i + 1) * Q)
            key = key_scr[rows, :]
            cand = jnp.where(b == 0, jnp.zeros_like(key), key | bit)
            cand_f = _key_to_f32(cand)
            hits = jnp.zeros((Q, LANES), F32)
            for c in range(i + 1):
                hits += jnp.where(score_scr[rows, c * Q:(c + 1) * Q] >= cand_f, 1.0, 0.0)
            cnt = _dot(hits.astype(BF16), ones)
            ok = cnt >= topk_f
            key_scr[rows, :] = jnp.where(ok, cand, key)
            cnt_scr[rows, :] = jnp.where(ok, cnt, cnt_scr[rows, :])
        return 0

    lax.fori_loop(0, 32, bit_step, 0)

    bound_scr[...] = jnp.full(bound_scr.shape, INT_MAX, jnp.int32)

    idx_bits = (seq - 1).bit_length()

    def tie_block(i, _):
        r0 = pl.multiple_of(i * Q, Q)

        @pl.when(jnp.max(cnt_scr[pl.ds(r0, Q), :]) > topk_f)
        def _():
            thr = _key_to_f32(key_scr[pl.ds(r0, Q), :])

            def count(pred_fn):
                def body(c, hits):
                    k0 = pl.multiple_of(c * Q, Q)
                    return hits + jnp.where(pred_fn(score_scr[pl.ds(r0, Q), pl.ds(k0, Q)], k0), 1.0, 0.0)
                return _dot(lax.fori_loop(0, i + 1, body, jnp.zeros((Q, LANES), F32)).astype(BF16), ones)

            need = topk_f - count(lambda sc, k0: sc > thr)

            def pos_step(b, bound):
                cand = bound - jnp.right_shift(jnp.int32(1 << (idx_bits - 1)), b)
                cnt = count(lambda sc, k0: (sc == thr) & (k0 + col_iota <= cand))
                return jnp.where(cnt >= need, cand, bound)

            all_keys = jnp.full((Q, LANES), (1 << idx_bits) - 1, jnp.int32)
            bound_scr[pl.ds(r0, Q), :] = lax.fori_loop(0, idx_bits, pos_step, all_keys)
        return 0

    lax.fori_loop(first, nb, tie_block, 0)

    twice = lambda a: jnp.concatenate([a, a], axis=1)

    def attn_block(blk, _):
        r0 = pl.multiple_of(blk * qa, qa)
        qlat = jnp.concatenate([qlat_ref[pl.ds(r0, qa), h * C_KV_LORA:(h + 1) * C_KV_LORA] for h in range(H)], axis=0)
        thr = _key_to_f32(key_scr[pl.ds(r0, qa), :])
        bound = bound_scr[pl.ds(r0, qa), :]
        m_scr[...] = jnp.full(m_scr.shape, jnp.finfo(F32).min, F32)
        l_scr[...] = jnp.zeros(l_scr.shape, F32)
        acc_scr[...] = jnp.zeros(acc_scr.shape, F32)

        def tile(k0, width):
            wide = lambda a: jnp.concatenate([a] * (width // LANES), axis=1)
            sc = score_scr[pl.ds(r0, qa), pl.ds(k0, width)]
            kidx = k0 + lax.broadcasted_iota(jnp.int32, (qa, width), 1)
            thr_w = wide(thr)
            sel = (sc > thr_w) | ((sc == thr_w) & (kidx <= wide(bound)))
            ckv = ckv_ref[pl.ds(k0, width), :]
            logits = _dot_nt(qlat, ckv)
            for h in range(H):
                rs = slice(h * qa, (h + 1) * qa)
                s = jnp.where(sel, logits[rs], -jnp.inf)
                m_old = m_scr[rs]
                m_new = jnp.maximum(m_old, jnp.max(s, axis=-1, keepdims=True))
                alpha = jnp.exp2(m_old - m_new)
                p = jnp.exp2(s - wide(m_new))
                l_scr[rs] = alpha * l_scr[rs] + jnp.sum(p, axis=-1, keepdims=True)
                acc_scr[rs] = twice(alpha) * acc_scr[rs] + _dot(p.astype(BF16), ckv)
                m_scr[rs] = m_new

        def tile_pair(cp, carry):
            tile(pl.multiple_of(cp * 2 * qa, 2 * qa), 2 * qa)
            return carry

        lax.fori_loop(0, (blk + 1) // 2, tile_pair, 0)

        @pl.when(blk % 2 == 0)
        def _():
            tile(pl.multiple_of(blk * qa, qa), qa)

        for h in range(H):
            rs = slice(h * qa, (h + 1) * qa)
            o_lat = (acc_scr[rs] / twice(l_scr[rs])).astype(BF16)
            o_ref[pl.ds(r0, qa), h * C_V:(h + 1) * C_V] = _dot(o_lat, w_uv_ref[h]).astype(BF16)
        return 0

    lax.fori_loop(0, seq // qa, attn_block, 0)


def _dsa_attn(qi, wi, qlat, ki, ckv, o, w_uv, batch):
    t = qi.shape[0]
    s = t // batch
    topk = min(TOPK_MAX, s // 4)
    assert topk % Q_BLOCK == 0 and s % (2 * Q_BLOCK) == 0 and s // Q_BLOCK <= 256
    seq = lambda w: pl.BlockSpec((s, w), lambda b: (b, 0), pipeline_mode=pl.Buffered(1))
    qa = 2 * Q_BLOCK
    return pl.pallas_call(
        functools.partial(_dsa_attn_body, topk, s),
        grid=(batch,),
        in_specs=[seq(IDX_HEADS * LANES), seq(LANES), seq(C_HEADS * C_KV_LORA), seq(LANES), seq(C_KV_LORA),
                  _stacked(w_uv, o)],
        out_specs=pl.BlockSpec((s, C_HEADS * C_V), lambda b: (b, 0)),
        out_shape=jax.ShapeDtypeStruct((t, C_HEADS * C_V), BF16),
        scratch_shapes=[pltpu.VMEM((s, s), F32),
                        pltpu.VMEM((s, LANES), jnp.int32),
                        pltpu.VMEM((s, LANES), F32),
                        pltpu.VMEM((s, LANES), jnp.int32),
                        pltpu.VMEM((C_HEADS * qa, LANES), F32),
                        pltpu.VMEM((C_HEADS * qa, LANES), F32),
                        pltpu.VMEM((C_HEADS * qa, C_KV_LORA), F32)],
        compiler_params=pltpu.CompilerParams(dimension_semantics=("arbitrary",), vmem_limit_bytes=DSA_VMEM_LIMIT),
        name="dsa_attn",
    )(qi, wi, qlat, ki, ckv, w_uv)


def _mem_kv_body(mem_ref, g_ref, wk_ref, wv_ref, k_ref, v_ref):
    mn = _rms(mem_ref[...], g_ref[...]).astype(BF16)
    k_ref[...] = _dot(mn, wk_ref[...]).astype(BF16)
    v_ref[...] = _dot(mn, wv_ref[...]).astype(BF16)


def _mem_kv(mem, g, wk, wv, n_mem):
    rows, d = mem.shape
    depth, _, w = wk.shape
    wspec = pl.BlockSpec((None, d, w), lambda l, b: (l, 0, 0))
    out = pl.BlockSpec((None, n_mem, w), lambda l, b: (l, b, 0))
    return pl.pallas_call(
        _mem_kv_body,
        grid=(depth, rows // n_mem),
        in_specs=[pl.BlockSpec((n_mem, d), lambda l, b: (b, 0)), pl.BlockSpec((None, 1, d), lambda l, b: (l, 0, 0)),
                  wspec, wspec],
        out_specs=[out, out],
        out_shape=[jax.ShapeDtypeStruct((depth, rows, w), BF16)] * 2,
        compiler_params=_params("parallel", "parallel"),
        name="mem_kv",
    )(mem, g, wk, wv)


def _xattn_val(x, gpre, gpost, wq_ref, wo_ref, k_ref, v_ref, att_scr):
    h = _rms(x, gpre).astype(BF16)
    q = (_dot(h, wq_ref[...]) * (X_HEAD_DIM ** -0.5 * LOG2E)).astype(BF16)
    for hd in range(X_HEADS):
        sl = slice(hd * X_HEAD_DIM, (hd + 1) * X_HEAD_DIM)
        s = _dot_nt(q[:, sl], k_ref[:, sl])
        p = jnp.exp2(s - jnp.max(s, axis=-1, keepdims=True))
        o = _dot(p.astype(BF16), v_ref[:, sl]) / jnp.sum(p, axis=-1, keepdims=True)
        att_scr[:, sl] = o.astype(BF16)
    return x + _rms(_dot(att_scr[...], wo_ref[...]), gpost)


def _post_body(n_in, x_ref, g_ref, *refs):
    a_refs, w_refs = refs[:n_in], refs[n_in:2 * n_in]
    wq_ref, wo_ref, k_ref, v_ref, wg_ref, wu_ref, wd_ref, o_ref, att_scr = refs[2 * n_in:]
    y = _dot(a_refs[0][...], w_refs[0][...])
    for a_ref, w_ref in zip(a_refs[1:], w_refs[1:]):
        y += _dot(a_ref[...], w_ref[...])
    x = x_ref[...] + _rms(y, g_ref[3:4, :])
    x = _xattn_val(x, g_ref[4:5, :], g_ref[5:6, :], wq_ref, wo_ref, k_ref, v_ref, att_scr)
    o_ref[...] = _ffn_val(x, g_ref[6:7, :], g_ref[7:8, :], wg_ref, wu_ref, wd_ref)


def _post_mixer(x, g, proj, xa_wq, xa_wo, mk, mv, wg, wu, wd, layer, batch, *, tm):
    t, d = x.shape
    acts = [a for a, _, _ in proj]
    w = xa_wq.shape[-1]
    f = wg.shape[-1]
    n_mem = mk.shape[1] // batch
    tiles_per_batch = t // batch // tm
    once = dict(pipeline_mode=pl.Buffered(1))
    resident = lambda r, c: pl.BlockSpec((None, None, r, c), lambda i: (layer, 1, 0, 0), **once)
    mem_spec = pl.BlockSpec((None, n_mem, w), lambda i: (layer, i // tiles_per_batch, 0))
    return pl.pallas_call(
        functools.partial(_post_body, len(proj)),
        grid=(t // tm,),
        in_specs=[pl.BlockSpec((tm, d), lambda i: (i, 0)), _stacked(g, layer)]
        + [pl.BlockSpec((tm, a.shape[1]), lambda i: (i, 0)) for a in acts]
        + [spec for _, _, spec in proj]
        + [_stacked(xa_wq, layer, **once), _stacked(xa_wo, layer, **once), mem_spec, mem_spec,
           resident(d, f), resident(d, f), resident(f, d)],
        out_specs=pl.BlockSpec((tm, d), lambda i: (i, 0)),
        out_shape=jax.ShapeDtypeStruct((t, d), F32),
        scratch_shapes=[pltpu.VMEM((tm, w), BF16)],
        compiler_params=pltpu.CompilerParams(dimension_semantics=("parallel",), vmem_limit_bytes=FUSED_VMEM_LIMIT),
        name="post_mixer",
    )(x, g, *acts, *[wt for _, wt, _ in proj], xa_wq, xa_wo, mk, mv, wg, wu, wd)


def _pad_last(a, width):
    return jnp.pad(a, [(0, 0)] * (a.ndim - 1) + [(0, width - a.shape[-1])])


def _prep_ab_weights(w_in, w_uq, w_ukv, w_gates, b_gates):
    n = w_in.shape[0]
    o = A_Q_LORA + A_KV_LORA
    w_in_p = jnp.concatenate([w_in[..., :o], _pad_last(w_in[..., o:o + A_ROPE], LANES), w_in[..., o + A_ROPE:]], axis=-1)
    uq = _pad_last(w_uq.reshape(n, A_Q_LORA, A_HEADS, A_NOPE + A_ROPE), A_HEAD_PAD)
    uq = uq.reshape(n, A_Q_LORA, A_HEADS * A_HEAD_PAD)
    ukv = w_ukv.reshape(n, A_KV_LORA, A_HEADS, A_NOPE + A_V)
    ukv = jnp.concatenate([ukv[..., :A_NOPE].reshape(n, A_KV_LORA, -1), ukv[..., A_NOPE:].reshape(n, A_KV_LORA, -1)],
                          axis=-1)
    return (w_in_p.astype(BF16), uq.astype(BF16), ukv.astype(BF16),
            _pad_last(w_gates, LANES).astype(BF16), _pad_last(b_gates[:, None, :], LANES))


def _prep_c_weights(w_in, w_uk):
    n = w_in.shape[0]
    o = C_Q_LORA + C_KV_LORA
    qi = _pad_last(w_in[..., o:o + IDX_HEADS * IDX_DIM].reshape(n, -1, IDX_HEADS, IDX_DIM), LANES)
    qi = qi.reshape(n, -1, IDX_HEADS * LANES)
    o2 = o + IDX_HEADS * IDX_DIM
    w_in_p = jnp.concatenate([w_in[..., :o], qi, _pad_last(w_in[..., o2:o2 + IDX_DIM], LANES),
                              _pad_last(w_in[..., o2 + IDX_DIM:], LANES)], axis=-1)
    return w_in_p.astype(BF16), jnp.swapaxes(w_uk, 2, 3).astype(BF16)


def kernel(x, mem, positions, norm_g, mem_norm_g, ffn_w_gate, ffn_w_up, ffn_w_down, xa_wq, xa_wk, xa_wv, xa_wo, ab_w_in, ab_w_out, mla_cq_g, mla_ckv_g, mla_w_uq, mla_w_ukv, ml_conv_w, ml_conv_b, ml_wq, ml_wk, ml_w_gates, ml_b_gates, ml_gn_g, ml_skip, c_w_in, c_w_out, c_cq_g, c_ckv_g, c_w_uq, c_w_uk, c_w_uv):
    batch, seq, d = x.shape
    depth = norm_g.shape[0]
    n_mem = mem.shape[1]
    t = batch * seq
    tm = min(512, seq)

    xt = x.reshape(t, d)
    memt = mem.reshape(batch * n_mem, d)
    pos = positions.reshape(t, 1).astype(F32)
    bf = lambda *ws: [w.astype(BF16) for w in ws]
    row3 = lambda a: a[:, None, :]
    once = dict(pipeline_mode=pl.Buffered(1))
    wg, wu, wd, xa_wq_b, xa_wk_b, xa_wv_b, xa_wo_b = bf(ffn_w_gate, ffn_w_up, ffn_w_down, xa_wq, xa_wk, xa_wv, xa_wo)
    ab_in, ab_uq, ab_ukv, ml_wgt, ml_bgt = _prep_ab_weights(ab_w_in, mla_w_uq, mla_w_ukv, ml_w_gates, ml_b_gates)
    c_in, c_ukt = _prep_c_weights(c_w_in, c_w_uk)
    ab_out, c_out, ml_wq_b, ml_wk_b, c_uq, c_uv = bf(ab_w_out, c_w_out, ml_wq, ml_wk, c_w_uq, c_w_uv)
    mk, mv = _mem_kv(memt, row3(mem_norm_g), xa_wk_b, xa_wv_b, n_mem)

    for l in range(depth):
        if l % 2 == 0:
            e = l // 2
            xt, q, k, v, xm, vm, og = _pre_ab(xt, pos, norm_g, wg, wu, wd, l, e, ab_in, row3(mla_cq_g), row3(mla_ckv_g),
                                              ab_uq, ab_ukv, tm=tm)
            ya = _mla_attn(q, k, v, batch, tq=min(256, seq))
            yb = _mlstm(xm, vm, og, e, ml_conv_w, row3(ml_conv_b), ml_wq_b, ml_wk_b, ml_wgt, ml_bgt,
                        row3(ml_gn_g), row3(ml_skip), batch)
            half = lambda part: pl.BlockSpec((None, A_HEADS * A_V, d), lambda i: (e, part, 0), **once)
            proj = [(ya, ab_out, half(0)), (yb, ab_out, half(1))]
        else:
            o = l // 2
            xt, qlat, ckv, qi, ki, wi = _pre_dsa(xt, norm_g, wg, wu, wd, l, o, c_in, row3(c_cq_g), row3(c_ckv_g),
                                                 c_uq, c_ukt, tm=tm)
            yc = _dsa_attn(qi, wi, qlat, ki, ckv, o, c_uv, batch)
            proj = [(yc, c_out, _stacked(c_out, o, **once))]
        xt = _post_mixer(xt, norm_g, proj, xa_wq_b, xa_wo_b, mk, mv, wg, wu, wd, l, batch, tm=tm)
    return xt.reshape(batch, seq, d)
```

```python
import functools
import math

import jax
import jax.numpy as jnp
from jax import lax
from jax.experimental import pallas as pl
from jax.experimental.pallas import tpu as pltpu

F32 = jnp.float32
BF16 = jnp.bfloat16

EPS = 1e-6
ROPE_BASE = 10000.0
LANES = 128
VMEM_LIMIT = 48 * 1024 * 1024

A_HEADS, A_Q_LORA, A_KV_LORA, A_NOPE, A_ROPE, A_V = 4, 384, 256, 128, 64, 128
B_HEADS, B_HEAD_DIM, B_CONV, B_CHUNK = 4, 128, 4, 128
B_WIDTH = B_HEADS * B_HEAD_DIM
C_HEADS, C_Q_LORA, C_KV_LORA, C_NOPE, C_V = 8, 384, 256, 128, 128
IDX_HEADS, IDX_DIM, TOPK_MAX = 8, 64, 256
X_HEADS, X_HEAD_DIM = 4, 128
Q_BLOCK = 128

LOG2E = math.log2(math.e)
INT_MIN = -(2 ** 31)
KEY_LOWEST_FINITE = INT_MIN + 0x00800000


def _params(*sem):
    return pltpu.CompilerParams(dimension_semantics=sem, vmem_limit_bytes=VMEM_LIMIT)


def _rms(x, g):
    return x * lax.rsqrt(jnp.mean(x * x, axis=-1, keepdims=True) + EPS) * g


def _dot(a, b):
    return jnp.dot(a, b, preferred_element_type=F32)


def _dot_nt(a, b):
    return lax.dot_general(a, b, (((1,), (1,)), ((), ())), preferred_element_type=F32)


def _dot_tn(a, b):
    return lax.dot_general(a, b, (((0,), (0,)), ((), ())), preferred_element_type=F32)


def _full(shape):
    return pl.BlockSpec(shape, lambda *_: (0,) * len(shape))


def _stacked(a, idx, **kw):
    return pl.BlockSpec((None,) + a.shape[1:], lambda *_: (idx,) + (0,) * (a.ndim - 1), **kw)


FUSED_VMEM_LIMIT = 56 * 1024 * 1024
MXU_DIM = 256


def _ffn_chunks(f):
    if f % MXU_DIM:
        return [(0, f)]
    first = (f // MXU_DIM + 1) // 2 * MXU_DIM
    return [(0, first), (first, f)] if first < f else [(0, f)]


def _ffn_val(x, gpre, gpost, wg_ref, wu_ref, wd_ref):
    h = _rms(x, gpre).astype(BF16)
    y = None
    for lo, hi in _ffn_chunks(wg_ref.shape[1]):
        cols = slice(lo, hi)
        g = _dot(h, wg_ref[:, cols])
        u = _dot(h, wu_ref[:, cols])
        a = (g * jax.nn.sigmoid(g) * u).astype(BF16)
        part = _dot(a, wd_ref[cols, :])
        y = part if y is None else y + part
    return x + 0.5 * _rms(y, gpost)


def _rope_tables(pos):
    half = A_ROPE // 2
    lane = lax.broadcasted_iota(jnp.int32, (1, LANES), 1)
    j = (lane % half).astype(F32)
    inv = jnp.exp(j * (-math.log(ROPE_BASE) / half))
    ang = pos * inv
    sin = jnp.where(lane < A_ROPE, jnp.sin(ang), 0.0)
    return jnp.cos(ang), sin


def _rope_apply(r, cos, sin):
    half = A_ROPE // 2
    return r * cos + (pltpu.roll(r, half, axis=1) - pltpu.roll(r, LANES - half, axis=1)) * sin


AB_Z = A_Q_LORA + A_KV_LORA + LANES + 3 * B_WIDTH
A_HEAD_PAD = 2 * LANES


def _ab_prep_val(x, pos, g, w_in_ref, cq_g_ref, ckv_g_ref, w_uq_ref, w_ukv_ref,
                 q_ref, k_ref, v_ref, xm_ref, vm_ref, og_ref):
    h = _rms(x, g).astype(BF16)
    z = _dot(h, w_in_ref[...])
    o = 0
    cq = z[:, o:o + A_Q_LORA]; o += A_Q_LORA
    ckv = z[:, o:o + A_KV_LORA]; o += A_KV_LORA
    kr = z[:, o:o + LANES]; o += LANES
    xm_ref[...] = z[:, o:o + B_WIDTH]; o += B_WIDTH
    vm_ref[...] = z[:, o:o + B_WIDTH]; o += B_WIDTH
    og_ref[...] = z[:, o:o + B_WIDTH]

    cos, sin = _rope_tables(pos)
    scale = (A_NOPE + A_ROPE) ** -0.5 * LOG2E
    q = _dot(_rms(cq, cq_g_ref[...]).astype(BF16), w_uq_ref[...]) * scale
    kv = _dot(_rms(ckv, ckv_g_ref[...]).astype(BF16), w_ukv_ref[...])
    k_rope = _rope_apply(kr, cos, sin).astype(BF16)
    for hd in range(A_HEADS):
        base = hd * A_HEAD_PAD
        q_ref[:, base:base + A_NOPE] = q[:, base:base + A_NOPE].astype(BF16)
        q_ref[:, base + A_NOPE:base + A_HEAD_PAD] = _rope_apply(
            q[:, base + A_NOPE:base + A_HEAD_PAD], cos, sin).astype(BF16)
        k_ref[:, base:base + A_NOPE] = kv[:, hd * A_NOPE:(hd + 1) * A_NOPE].astype(BF16)
        k_ref[:, base + A_NOPE:base + A_HEAD_PAD] = k_rope
    v_ref[...] = kv[:, A_HEADS * A_NOPE:].astype(BF16)


def _pre_ab_body(x_ref, pos_ref, g_ref, wg_ref, wu_ref, wd_ref, w_in_ref, cq_g_ref, ckv_g_ref, w_uq_ref,
                 w_ukv_ref, xo_ref, q_ref, k_ref, v_ref, xm_ref, vm_ref, og_ref):
    x = _ffn_val(x_ref[...], g_ref[0:1, :], g_ref[1:2, :], wg_ref, wu_ref, wd_ref)
    xo_ref[...] = x
    _ab_prep_val(x, pos_ref[...], g_ref[2:3, :], w_in_ref, cq_g_ref, ckv_g_ref, w_uq_ref, w_ukv_ref,
                 q_ref, k_ref, v_ref, xm_ref, vm_ref, og_ref)


def _pre_ab(x, pos, g, wg, wu, wd, layer, e, w_in, cq_g, ckv_g, w_uq, w_ukv, *, tm):
    t, d = x.shape
    f = wg.shape[-1]
    once = dict(pipeline_mode=pl.Buffered(1))
    resident = lambda r, c: pl.BlockSpec((None, None, r, c), lambda i: (layer, 0, 0, 0), **once)
    const = lambda a: _stacked(a, e, **once)
    row = lambda w: pl.BlockSpec((tm, w), lambda i: (i, 0))
    return pl.pallas_call(
        _pre_ab_body,
        grid=(t // tm,),
        in_specs=[row(d), row(1), _stacked(g, layer), resident(d, f), resident(d, f), resident(f, d),
                  const(w_in), const(cq_g), const(ckv_g), const(w_uq), const(w_ukv)],
        out_specs=[row(d), row(A_HEADS * A_HEAD_PAD), row(A_HEADS * A_HEAD_PAD), row(A_HEADS * A_V),
                   row(B_WIDTH), row(B_WIDTH), row(B_WIDTH)],
        out_shape=[jax.ShapeDtypeStruct((t, d), F32),
                   jax.ShapeDtypeStruct((t, A_HEADS * A_HEAD_PAD), BF16),
                   jax.ShapeDtypeStruct((t, A_HEADS * A_HEAD_PAD), BF16),
                   jax.ShapeDtypeStruct((t, A_HEADS * A_V), BF16),
                   jax.ShapeDtypeStruct((t, B_WIDTH), F32),
                   jax.ShapeDtypeStruct((t, B_WIDTH), F32),
                   jax.ShapeDtypeStruct((t, B_WIDTH), F32)],
        compiler_params=pltpu.CompilerParams(dimension_semantics=("parallel",), vmem_limit_bytes=FUSED_VMEM_LIMIT),
        name="pre_ab",
    )(x, pos, g, wg, wu, wd, w_in, cq_g, ckv_g, w_uq, w_ukv)


def _mla_attn_body(tq, q_ref, k_ref, v_ref, o_ref, m_scr, l_scr, acc_scr):
    i = pl.program_id(1)
    heads = range(A_HEADS)
    m_scr[...] = jnp.full(m_scr.shape, -jnp.inf, F32)
    l_scr[...] = jnp.zeros(l_scr.shape, F32)
    acc_scr[...] = jnp.zeros(acc_scr.shape, F32)

    def chunk(start, width, masked):
        widen = lambda a: jnp.concatenate([a] * (width // LANES), axis=1)
        for h in heads:
            s = _dot_nt(q_ref[:, h * A_HEAD_PAD:(h + 1) * A_HEAD_PAD],
                        k_ref[pl.ds(start, width), h * A_HEAD_PAD:(h + 1) * A_HEAD_PAD])
            if masked:
                row = lax.broadcasted_iota(jnp.int32, (tq, width), 0)
                col = lax.broadcasted_iota(jnp.int32, (tq, width), 1)
                s = jnp.where(col <= row, s, -jnp.inf)
            m_old = m_scr[h]
            m_new = jnp.maximum(m_old, jnp.max(s, axis=-1, keepdims=True))
            alpha = jnp.exp2(m_old - m_new)
            p = jnp.exp2(s - widen(m_new))
            l_scr[h] = alpha * l_scr[h] + jnp.sum(p, axis=-1, keepdims=True)
            acc_scr[h] = alpha * acc_scr[h] + _dot(p.astype(BF16), v_ref[pl.ds(start, width), h * A_V:(h + 1) * A_V])
            m_scr[h] = m_new

    def chunk_pair(jp, carry):
        chunk(pl.multiple_of(jp * 2 * tq, 2 * tq), 2 * tq, False)
        return carry

    lax.fori_loop(0, i // 2, chunk_pair, 0)

    @pl.when(i % 2 == 1)
    def _():
        chunk(pl.multiple_of((i - 1) * tq, tq), tq, False)

    chunk(pl.multiple_of(i * tq, tq), tq, True)
    for h in heads:
        o_ref[:, h * A_V:(h + 1) * A_V] = (acc_scr[h] / l_scr[h]).astype(BF16)


def _mla_attn(q, k, v, batch, *, tq):
    t = q.shape[0]
    s = t // batch
    nq = s // tq
    return pl.pallas_call(
        functools.partial(_mla_attn_body, tq),
        grid=(batch, nq),
        in_specs=[pl.BlockSpec((tq, A_HEADS * A_HEAD_PAD), lambda b, i: (b * nq + i, 0)),
                  pl.BlockSpec((s, A_HEADS * A_HEAD_PAD), lambda b, i: (b, 0)),
                  pl.BlockSpec((s, A_HEADS * A_V), lambda b, i: (b, 0))],
        out_specs=pl.BlockSpec((tq, A_HEADS * A_V), lambda b, i: (b * nq + i, 0)),
        out_shape=jax.ShapeDtypeStruct((t, A_HEADS * A_V), BF16),
        scratch_shapes=[pltpu.VMEM((A_HEADS, tq, LANES), F32), pltpu.VMEM((A_HEADS, tq, LANES), F32),
                        pltpu.VMEM((A_HEADS, tq, A_V), F32)],
        compiler_params=_params("parallel", "arbitrary"),
        name="mla_attn",
    )(q, k, v)


def _log_sigmoid(x):
    return jnp.minimum(x, 0.0) - jnp.log(1.0 + jnp.exp(-jnp.abs(x)))


def _mlstm_body(xm_ref, vm_ref, og_ref, conv_w_ref, conv_b_ref, wq_ref, wk_ref, wgt_ref, bgt_ref,
                gn_ref, skip_ref, y_ref, *scratch):
    c = pl.program_id(1)
    L = B_CHUNK
    tail = 8
    per_elem = [scratch[4 * e:4 * e + 4] for e in range(xm_ref.shape[0])]

    @pl.when(c == 0)
    def _():
        for xext_scr, _, state_scr, m_scr in per_elem:
            xext_scr[0:tail, :] = jnp.zeros((tail, B_WIDTH), F32)
            state_scr[...] = jnp.zeros_like(state_scr)
            m_scr[...] = jnp.zeros_like(m_scr)

    lane = lax.broadcasted_iota(jnp.int32, (L, LANES), 1)
    row = lax.broadcasted_iota(jnp.int32, (L, LANES), 0)
    is_f = (lane >= B_HEADS) & (lane < 2 * B_HEADS)
    causal = lax.broadcasted_iota(jnp.int32, (L, L), 1) <= lax.broadcasted_iota(jnp.int32, (L, L), 0)
    for e, (xext_scr, gin_scr, state_scr, m_scr) in enumerate(per_elem):
        _mlstm_chunk(xm_ref.at[e], vm_ref.at[e], og_ref.at[e], conv_w_ref, conv_b_ref, wq_ref, wk_ref, wgt_ref,
                     bgt_ref, gn_ref, skip_ref, y_ref.at[e], xext_scr, gin_scr, state_scr, m_scr,
                     row, is_f, causal)


def _mlstm_chunk(xm_ref, vm_ref, og_ref, conv_w_ref, conv_b_ref, wq_ref, wk_ref, wgt_ref, bgt_ref,
                 gn_ref, skip_ref, y_ref, xext_scr, gin_scr, state_scr, m_scr, row, is_f, causal):
    L = B_CHUNK
    tail = 8
    xm = xm_ref[...]
    xext_scr[tail:tail + L, :] = xm
    conv = conv_b_ref[...] + conv_w_ref[B_CONV - 1:B_CONV, :] * xm
    for tap in range(B_CONV - 1):
        shift = B_CONV - 1 - tap
        conv += conv_w_ref[tap:tap + 1, :] * xext_scr[tail - shift:tail - shift + L, :]
    xext_scr[0:tail, :] = xm[L - tail:, :]
    xc = conv * jax.nn.sigmoid(conv)
    xcb = xc.astype(BF16)

    vm = vm_ref[...]
    qs, ks = [], []
    for h in range(B_HEADS):
        sl = slice(h * B_HEAD_DIM, (h + 1) * B_HEAD_DIM)
        qh = _dot(xcb[:, sl], wq_ref[h])
        kh = _dot(xcb[:, sl], wk_ref[h])
        qs.append(qh)
        ks.append(kh)
        gin_scr[:, h * B_HEAD_DIM:(h + 1) * B_HEAD_DIM] = qh.astype(BF16)
        gin_scr[:, B_WIDTH + h * B_HEAD_DIM:B_WIDTH + (h + 1) * B_HEAD_DIM] = kh.astype(BF16)
    gin_scr[:, 2 * B_WIDTH:] = vm.astype(BF16)
    gates = _dot(gin_scr[...], wgt_ref[...]) + bgt_ref[...]

    glog = jnp.where(is_f, _log_sigmoid(gates), gates)
    bcum = jnp.where(is_f, glog, 0.0)
    step = 1
    while step < L:
        bcum = bcum + jnp.where(row >= step, pltpu.roll(bcum, step, axis=0), 0.0)
        step *= 2
    glog_t = glog.T
    bcum_t = bcum.T

    kscale = B_HEAD_DIM ** -0.5
    ones_blk = jnp.ones((L, LANES), BF16)
    twice = lambda a: jnp.concatenate([a, a], axis=1)
    for h in range(B_HEADS):
        sl = slice(h * B_HEAD_DIM, (h + 1) * B_HEAD_DIM)
        qb = qs[h].astype(BF16)
        k_s = ks[h] * kscale
        v_ext = jnp.concatenate([vm[:, sl].astype(BF16), ones_blk], axis=1)
        li_c = jnp.broadcast_to(glog[:, h:h + 1], (L, LANES))
        b_c = jnp.broadcast_to(bcum[:, B_HEADS + h:B_HEADS + h + 1], (L, LANES))
        li_r = glog_t[h:h + 1, :]
        b_r = bcum_t[B_HEADS + h:B_HEADS + h + 1, :]
        b_tot = jnp.broadcast_to(b_r[:, L - 1:L], (1, LANES))
        m_st = m_scr[h:h + 1, :]
        state = state_scr[h]

        dmat = jnp.where(causal, b_c - b_r + li_r, -jnp.inf)
        m_inter = b_c + m_st
        m_j = jnp.maximum(m_inter, jnp.max(dmat, axis=-1, keepdims=True))
        sc = _dot_nt(qb, k_s.astype(BF16)) * jnp.exp(dmat - m_j)
        inter = jnp.exp(m_inter - m_j)
        ne = twice(inter) * _dot(qb, state.astype(BF16)) + _dot(sc.astype(BF16), v_ext)
        hj = ne[:, :B_HEAD_DIM] / jnp.maximum(jnp.abs(ne[:, B_HEAD_DIM:]), jnp.exp(-m_j))

        g_c = b_tot - b_c + li_c
        g_r = b_tot - b_r + li_r
        m_new = jnp.maximum(b_tot + m_st, jnp.max(g_r, axis=-1, keepdims=True))
        decay = jnp.exp(b_tot + m_st - m_new)
        kw = (k_s * jnp.exp(g_c - m_new)).astype(BF16)
        state_scr[h] = twice(decay) * state + _dot_tn(kw, v_ext)
        m_scr[h:h + 1, :] = m_new

        mu = jnp.mean(hj, axis=-1, keepdims=True)
        var = jnp.mean(jnp.square(hj - mu), axis=-1, keepdims=True)
        hn = (hj - mu) * lax.rsqrt(var + EPS)
        out = (hn * gn_ref[:, sl] + skip_ref[:, sl] * xc[:, sl]) * jax.nn.sigmoid(og_ref[:, sl])
        y_ref[:, sl] = out.astype(BF16)


def _mlstm(xm, vm, og, e, conv_w, conv_b, wq, wk, wgt, bgt, gn_g, skip, batch):
    t = xm.shape[0]
    s = t // batch
    per_step = 2 if batch % 2 == 0 else 1
    seq3 = lambda a: a.reshape(batch, s, B_WIDTH)
    row = pl.BlockSpec((per_step, B_CHUNK, B_WIDTH), lambda b, c: (b, c, 0))
    weights = (conv_w, conv_b, wq, wk, wgt, bgt, gn_g, skip)
    y = pl.pallas_call(
        _mlstm_body,
        grid=(batch // per_step, s // B_CHUNK),
        in_specs=[row, row, row] + [_stacked(a, e) for a in weights],
        out_specs=row,
        out_shape=jax.ShapeDtypeStruct((batch, s, B_WIDTH), BF16),
        scratch_shapes=[pltpu.VMEM((8 + B_CHUNK, B_WIDTH), F32),
                        pltpu.VMEM((B_CHUNK, 3 * B_WIDTH), BF16),
                        pltpu.VMEM((B_HEADS, B_HEAD_DIM, 2 * LANES), F32),
                        pltpu.VMEM((8, LANES), F32)] * per_step,
        compiler_params=_params("parallel", "arbitrary"),
        name="mlstm",
    )(seq3(xm), seq3(vm), seq3(og), *weights)
    return y.reshape(t, B_WIDTH)


C_Z = C_Q_LORA + C_KV_LORA + IDX_HEADS * LANES + LANES + LANES


def _dsa_prep_val(x, g, w_in_ref, cq_g_ref, ckv_g_ref, w_uq_ref, w_ukt_ref,
                  qlat_ref, ckv_ref, qi_ref, ki_ref, wi_ref):
    h = _rms(x, g).astype(BF16)
    z = _dot(h, w_in_ref[...])
    o = 0
    cq = z[:, o:o + C_Q_LORA]; o += C_Q_LORA
    ckv = z[:, o:o + C_KV_LORA]; o += C_KV_LORA
    qi_ref[...] = z[:, o:o + IDX_HEADS * LANES].astype(BF16); o += IDX_HEADS * LANES
    ki_ref[...] = z[:, o:o + LANES].astype(BF16); o += LANES
    wi_ref[...] = z[:, o:o + LANES]
    ckv_ref[...] = _rms(ckv, ckv_g_ref[...]).astype(BF16)
    q = _dot(_rms(cq, cq_g_ref[...]).astype(BF16), w_uq_ref[...]).astype(BF16)
    scale = C_NOPE ** -0.5 * LOG2E
    for hd in range(C_HEADS):
        ql = _dot(q[:, hd * C_NOPE:(hd + 1) * C_NOPE], w_ukt_ref[hd]) * scale
        qlat_ref[:, hd * C_KV_LORA:(hd + 1) * C_KV_LORA] = ql.astype(BF16)


def _pre_dsa_body(x_ref, g_ref, wg_ref, wu_ref, wd_ref, w_in_ref, cq_g_ref, ckv_g_ref, w_uq_ref, w_ukt_ref,
                  xo_ref, qlat_ref, ckv_ref, qi_ref, ki_ref, wi_ref):
    x = _ffn_val(x_ref[...], g_ref[0:1, :], g_ref[1:2, :], wg_ref, wu_ref, wd_ref)
    xo_ref[...] = x
    _dsa_prep_val(x, g_ref[2:3, :], w_in_ref, cq_g_ref, ckv_g_ref, w_uq_ref, w_ukt_ref,
                  qlat_ref, ckv_ref, qi_ref, ki_ref, wi_ref)


def _pre_dsa(x, g, wg, wu, wd, layer, o, w_in, cq_g, ckv_g, w_uq, w_ukt, *, tm):
    t, d = x.shape
    f = wg.shape[-1]
    once = dict(pipeline_mode=pl.Buffered(1))
    resident = lambda r, c: pl.BlockSpec((None, None, r, c), lambda i: (layer, 0, 0, 0), **once)
    const = lambda a: _stacked(a, o, **once)
    row = lambda w: pl.BlockSpec((tm, w), lambda i: (i, 0))
    return pl.pallas_call(
        _pre_dsa_body,
        grid=(t // tm,),
        in_specs=[row(d), _stacked(g, layer), resident(d, f), resident(d, f), resident(f, d),
                  const(w_in), const(cq_g), const(ckv_g), const(w_uq), const(w_ukt)],
        out_specs=[row(d), row(C_HEADS * C_KV_LORA), row(C_KV_LORA), row(IDX_HEADS * LANES), row(LANES), row(LANES)],
        out_shape=[jax.ShapeDtypeStruct((t, d), F32),
                   jax.ShapeDtypeStruct((t, C_HEADS * C_KV_LORA), BF16),
                   jax.ShapeDtypeStruct((t, C_KV_LORA), BF16),
                   jax.ShapeDtypeStruct((t, IDX_HEADS * LANES), BF16),
                   jax.ShapeDtypeStruct((t, LANES), BF16),
                   jax.ShapeDtypeStruct((t, LANES), F32)],
        compiler_params=pltpu.CompilerParams(dimension_semantics=("parallel",), vmem_limit_bytes=FUSED_VMEM_LIMIT),
        name="pre_dsa",
    )(x, g, wg, wu, wd, w_in, cq_g, ckv_g, w_uq, w_ukt)


INT_MAX = 2 ** 31 - 1
DSA_VMEM_LIMIT = 56 * 1024 * 1024


def _key_to_f32(key):
    return pltpu.bitcast(key ^ ((key >> 31) & 0x7FFFFFFF), F32)


def _dsa_attn_body(topk, seq, qi_ref, wi_ref, qlat_ref, ki_ref, ckv_ref, w_uv_ref, o_ref,
                   score_scr, key_scr, cnt_scr, bound_scr, m_scr, l_scr, acc_scr):
    Q = Q_BLOCK
    H = C_HEADS
    nb = seq // Q
    first = topk // Q
    col_iota = lax.broadcasted_iota(jnp.int32, (Q, Q), 1)
    ones = jnp.ones((Q, LANES), BF16)
    topk_f = float(topk)

    qa = 2 * Q
    qa_col = lax.broadcasted_iota(jnp.int32, (qa, qa), 1)
    qa_row = lax.broadcasted_iota(jnp.int32, (qa, qa), 0)

    def score_block(blk, _):
        r0 = pl.multiple_of(blk * qa, qa)
        wi = wi_ref[pl.ds(r0, qa), :]
        wb = [jnp.broadcast_to(wi[:, h:h + 1], (qa, qa)) for h in range(IDX_HEADS)]

        def tile(c, _):
            k0 = pl.multiple_of(c * qa, qa)
            keys = ki_ref[pl.ds(k0, qa), :]
            score = None
            for h in range(IDX_HEADS):
                rel = jnp.maximum(_dot_nt(qi_ref[pl.ds(r0, qa), h * LANES:(h + 1) * LANES], keys), 0.0)
                score = rel * wb[h] if score is None else score + rel * wb[h]
            score = jnp.where(k0 + qa_col <= r0 + qa_row, score, -jnp.inf)
            score_scr[pl.ds(r0, qa), pl.ds(k0, qa)] = score
            return 0

        lax.fori_loop(0, blk + 1, tile, 0)
        return 0

    lax.fori_loop(0, seq // qa, score_block, 0)

    if first > 0:
        key_scr[0:first * Q, :] = jnp.full((first * Q, LANES), KEY_LOWEST_FINITE, jnp.int32)
        cnt_scr[0:first * Q, :] = jnp.full((first * Q, LANES), topk_f, F32)
    for i in range(first, nb):
        key_scr[i * Q:(i + 1) * Q, :] = jnp.full((Q, LANES), INT_MIN, jnp.int32)
        cnt_scr[i * Q:(i + 1) * Q, :] = jnp.full((Q, LANES), float((i + 1) * Q), F32)

    def bit_step(b, _):
        bit = jnp.left_shift(jnp.int32(1), 31 - b)
        for i in range(first, nb):
            rows = slice(i * Q, (i + 1) * Q)
            key = key_scr[rows, :]
            cand = jnp.where(b == 0, jnp.zeros_like(key), key | bit)
            cand_f = _key_to_f32(cand)
            hits = jnp.zeros((Q, LANES), F32)
            for c in range(i + 1):
                hits += jnp.where(score_scr[rows, c * Q:(c + 1) * Q] >= cand_f, 1.0, 0.0)
            cnt = _dot(hits.astype(BF16), ones)
            ok = cnt >= topk_f
            key_scr[rows, :] = jnp.where(ok, cand, key)
            cnt_scr[rows, :] = jnp.where(ok, cnt, cnt_scr[rows, :])
        return 0

    lax.fori_loop(0, 32, bit_step, 0)

    bound_scr[...] = jnp.full(bound_scr.shape, INT_MAX, jnp.int32)

    idx_bits = (seq - 1).bit_length()

    def tie_block(i, _):
        r0 = pl.multiple_of(i * Q, Q)

        @pl.when(jnp.max(cnt_scr[pl.ds(r0, Q), :]) > topk_f)
        def _():
            thr = _key_to_f32(key_scr[pl.ds(r0, Q), :])

            def count(pred_fn):
                def body(c, hits):
                    k0 = pl.multiple_of(c * Q, Q)
                    return hits + jnp.where(pred_fn(score_scr[pl.ds(r0, Q), pl.ds(k0, Q)], k0), 1.0, 0.0)
                return _dot(lax.fori_loop(0, i + 1, body, jnp.zeros((Q, LANES), F32)).astype(BF16), ones)

            need = topk_f - count(lambda sc, k0: sc > thr)

            def pos_step(b, bound):
                cand = bound - jnp.right_shift(jnp.int32(1 << (idx_bits - 1)), b)
                cnt = count(lambda sc, k0: (sc == thr) & (k0 + col_iota <= cand))
                return jnp.where(cnt >= need, cand, bound)

            all_keys = jnp.full((Q, LANES), (1 << idx_bits) - 1, jnp.int32)
            bound_scr[pl.ds(r0, Q), :] = lax.fori_loop(0, idx_bits, pos_step, all_keys)
        return 0

    lax.fori_loop(first, nb, tie_block, 0)

    twice = lambda a: jnp.concatenate([a, a], axis=1)

    def attn_block(blk, _):
        r0 = pl.multiple_of(blk * qa, qa)
        qlat = jnp.concatenate([qlat_ref[pl.ds(r0, qa), h * C_KV_LORA:(h + 1) * C_KV_LORA] for h in range(H)], axis=0)
        thr = _key_to_f32(key_scr[pl.ds(r0, qa), :])
        bound = bound_scr[pl.ds(r0, qa), :]
        m_scr[...] = jnp.full(m_scr.shape, jnp.finfo(F32).min, F32)
        l_scr[...] = jnp.zeros(l_scr.shape, F32)
        acc_scr[...] = jnp.zeros(acc_scr.shape, F32)

        def tile(k0, width):
            wide = lambda a: jnp.concatenate([a] * (width // LANES), axis=1)
            sc = score_scr[pl.ds(r0, qa), pl.ds(k0, width)]
            kidx = k0 + lax.broadcasted_iota(jnp.int32, (qa, width), 1)
            thr_w = wide(thr)
            sel = (sc > thr_w) | ((sc == thr_w) & (kidx <= wide(bound)))
            ckv = ckv_ref[pl.ds(k0, width), :]
            logits = _dot_nt(qlat, ckv)
            for h in range(H):
                rs = slice(h * qa, (h + 1) * qa)
                s = jnp.where(sel, logits[rs], -jnp.inf)
                m_old = m_scr[rs]
                m_new = jnp.maximum(m_old, jnp.max(s, axis=-1, keepdims=True))
                alpha = jnp.exp2(m_old - m_new)
                p = jnp.exp2(s - wide(m_new))
                l_scr[rs] = alpha * l_scr[rs] + jnp.sum(p, axis=-1, keepdims=True)
                acc_scr[rs] = twice(alpha) * acc_scr[rs] + _dot(p.astype(BF16), ckv)
                m_scr[rs] = m_new

        def tile_pair(cp, carry):
            tile(pl.multiple_of(cp * 2 * qa, 2 * qa), 2 * qa)
            return carry

        lax.fori_loop(0, (blk + 1) // 2, tile_pair, 0)

        @pl.when(blk % 2 == 0)
        def _():
            tile(pl.multiple_of(blk * qa, qa), qa)

        for h in range(H):
            rs = slice(h * qa, (h + 1) * qa)
            o_lat = (acc_scr[rs] / twice(l_scr[rs])).astype(BF16)
            o_ref[pl.ds(r0, qa), h * C_V:(h + 1) * C_V] = _dot(o_lat, w_uv_ref[h]).astype(BF16)
        return 0

    lax.fori_loop(0, seq // qa, attn_block, 0)


def _dsa_attn(qi, wi, qlat, ki, ckv, o, w_uv, batch):
    t = qi.shape[0]
    s = t // batch
    topk = min(TOPK_MAX, s // 4)
    assert topk % Q_BLOCK == 0 and s % (2 * Q_BLOCK) == 0 and s // Q_BLOCK <= 256
    seq = lambda w: pl.BlockSpec((s, w), lambda b: (b, 0), pipeline_mode=pl.Buffered(1))
    qa = 2 * Q_BLOCK
    return pl.pallas_call(
        functools.partial(_dsa_attn_body, topk, s),
        grid=(batch,),
        in_specs=[seq(IDX_HEADS * LANES), seq(LANES), seq(C_HEADS * C_KV_LORA), seq(LANES), seq(C_KV_LORA),
                  _stacked(w_uv, o)],
        out_specs=pl.BlockSpec((s, C_HEADS * C_V), lambda b: (b, 0)),
        out_shape=jax.ShapeDtypeStruct((t, C_HEADS * C_V), BF16),
        scratch_shapes=[pltpu.VMEM((s, s), F32),
                        pltpu.VMEM((s, LANES), jnp.int32),
                        pltpu.VMEM((s, LANES), F32),
                        pltpu.VMEM((s, LANES), jnp.int32),
                        pltpu.VMEM((C_HEADS * qa, LANES), F32),
                        pltpu.VMEM((C_HEADS * qa, LANES), F32),
                        pltpu.VMEM((C_HEADS * qa, C_KV_LORA), F32)],
        compiler_params=pltpu.CompilerParams(dimension_semantics=("arbitrary",), vmem_limit_bytes=DSA_VMEM_LIMIT),
        name="dsa_attn",
    )(qi, wi, qlat, ki, ckv, w_uv)


def _mem_kv_body(mem_ref, g_ref, wk_ref, wv_ref, k_ref, v_ref):
    mn = _rms(mem_ref[...], g_ref[...]).astype(BF16)
    k_ref[...] = _dot(mn, wk_ref[...]).astype(BF16)
    v_ref[...] = _dot(mn, wv_ref[...]).astype(BF16)


def _mem_kv(mem, g, wk, wv, n_mem):
    rows, d = mem.shape
    depth, _, w = wk.shape
    wspec = pl.BlockSpec((None, d, w), lambda l, b: (l, 0, 0))
    out = pl.BlockSpec((None, n_mem, w), lambda l, b: (l, b, 0))
    return pl.pallas_call(
        _mem_kv_body,
        grid=(depth, rows // n_mem),
        in_specs=[pl.BlockSpec((n_mem, d), lambda l, b: (b, 0)), pl.BlockSpec((None, 1, d), lambda l, b: (l, 0, 0)),
                  wspec, wspec],
        out_specs=[out, out],
        out_shape=[jax.ShapeDtypeStruct((depth, rows, w), BF16)] * 2,
        compiler_params=_params("parallel", "parallel"),
        name="mem_kv",
    )(mem, g, wk, wv)


def _xattn_val(x, gpre, gpost, wq_ref, wo_ref, k_ref, v_ref, att_scr):
    h = _rms(x, gpre).astype(BF16)
    q = (_dot(h, wq_ref[...]) * (X_HEAD_DIM ** -0.5 * LOG2E)).astype(BF16)
    for hd in range(X_HEADS):
        sl = slice(hd * X_HEAD_DIM, (hd + 1) * X_HEAD_DIM)
        s = _dot_nt(q[:, sl], k_ref[:, sl])
        p = jnp.exp2(s - jnp.max(s, axis=-1, keepdims=True))
        o = _dot(p.astype(BF16), v_ref[:, sl]) / jnp.sum(p, axis=-1, keepdims=True)
        att_scr[:, sl] = o.astype(BF16)
    return x + _rms(_dot(att_scr[...], wo_ref[...]), gpost)


def _post_body(n_in, x_ref, g_ref, *refs):
    a_refs, w_refs = refs[:n_in], refs[n_in:2 * n_in]
    wq_ref, wo_ref, k_ref, v_ref, wg_ref, wu_ref, wd_ref, o_ref, att_scr = refs[2 * n_in:]
    y = _dot(a_refs[0][...], w_refs[0][...])
    for a_ref, w_ref in zip(a_refs[1:], w_refs[1:]):
        y += _dot(a_ref[...], w_ref[...])
    x = x_ref[...] + _rms(y, g_ref[3:4, :])
    x = _xattn_val(x, g_ref[4:5, :], g_ref[5:6, :], wq_ref, wo_ref, k_ref, v_ref, att_scr)
    o_ref[...] = _ffn_val(x, g_ref[6:7, :], g_ref[7:8, :], wg_ref, wu_ref, wd_ref)


def _post_mixer(x, g, proj, xa_wq, xa_wo, mk, mv, wg, wu, wd, layer, batch, *, tm):
    t, d = x.shape
    acts = [a for a, _, _ in proj]
    w = xa_wq.shape[-1]
    f = wg.shape[-1]
    n_mem = mk.shape[1] // batch
    tiles_per_batch = t // batch // tm
    once = dict(pipeline_mode=pl.Buffered(1))
    resident = lambda r, c: pl.BlockSpec((None, None, r, c), lambda i: (layer, 1, 0, 0), **once)
    mem_spec = pl.BlockSpec((None, n_mem, w), lambda i: (layer, i // tiles_per_batch, 0))
    return pl.pallas_call(
        functools.partial(_post_body, len(proj)),
        grid=(t // tm,),
        in_specs=[pl.BlockSpec((tm, d), lambda i: (i, 0)), _stacked(g, layer)]
        + [pl.BlockSpec((tm, a.shape[1]), lambda i: (i, 0)) for a in acts]
        + [spec for _, _, spec in proj]
        + [_stacked(xa_wq, layer, **once), _stacked(xa_wo, layer, **once), mem_spec, mem_spec,
           resident(d, f), resident(d, f), resident(f, d)],
        out_specs=pl.BlockSpec((tm, d), lambda i: (i, 0)),
        out_shape=jax.ShapeDtypeStruct((t, d), F32),
        scratch_shapes=[pltpu.VMEM((tm, w), BF16)],
        compiler_params=pltpu.CompilerParams(dimension_semantics=("parallel",), vmem_limit_bytes=FUSED_VMEM_LIMIT),
        name="post_mixer",
    )(x, g, *acts, *[wt for _, wt, _ in proj], xa_wq, xa_wo, mk, mv, wg, wu, wd)


def _pad_last(a, width):
    return jnp.pad(a, [(0, 0)] * (a.ndim - 1) + [(0, width - a.shape[-1])])


def _prep_ab_weights(w_in, w_uq, w_ukv, w_gates, b_gates):
    n = w_in.shape[0]
    o = A_Q_LORA + A_KV_LORA
    w_in_p = jnp.concatenate([w_in[..., :o], _pad_last(w_in[..., o:o + A_ROPE], LANES), w_in[..., o + A_ROPE:]], axis=-1)
    uq = _pad_last(w_uq.reshape(n, A_Q_LORA, A_HEADS, A_NOPE + A_ROPE), A_HEAD_PAD)
    uq = uq.reshape(n, A_Q_LORA, A_HEADS * A_HEAD_PAD)
    ukv = w_ukv.reshape(n, A_KV_LORA, A_HEADS, A_NOPE + A_V)
    ukv = jnp.concatenate([ukv[..., :A_NOPE].reshape(n, A_KV_LORA, -1), ukv[..., A_NOPE:].reshape(n, A_KV_LORA, -1)],
                          axis=-1)
    return (w_in_p.astype(BF16), uq.astype(BF16), ukv.astype(BF16),
            _pad_last(w_gates, LANES).astype(BF16), _pad_last(b_gates[:, None, :], LANES))


def _prep_c_weights(w_in, w_uk):
    n = w_in.shape[0]
    o = C_Q_LORA + C_KV_LORA
    qi = _pad_last(w_in[..., o:o + IDX_HEADS * IDX_DIM].reshape(n, -1, IDX_HEADS, IDX_DIM), LANES)
    qi = qi.reshape(n, -1, IDX_HEADS * LANES)
    o2 = o + IDX_HEADS * IDX_DIM
    w_in_p = jnp.concatenate([w_in[..., :o], qi, _pad_last(w_in[..., o2:o2 + IDX_DIM], LANES),
                              _pad_last(w_in[..., o2 + IDX_DIM:], LANES)], axis=-1)
    return w_in_p.astype(BF16), jnp.swapaxes(w_uk, 2, 3).astype(BF16)


def kernel(x, mem, positions, norm_g, mem_norm_g, ffn_w_gate, ffn_w_up, ffn_w_down, xa_wq, xa_wk, xa_wv, xa_wo, ab_w_in, ab_w_out, mla_cq_g, mla_ckv_g, mla_w_uq, mla_w_ukv, ml_conv_w, ml_conv_b, ml_wq, ml_wk, ml_w_gates, ml_b_gates, ml_gn_g, ml_skip, c_w_in, c_w_out, c_cq_g, c_ckv_g, c_w_uq, c_w_uk, c_w_uv):
    batch, seq, d = x.shape
    depth = norm_g.shape[0]
    n_mem = mem.shape[1]
    t = batch * seq
    tm = min(512, seq)

    xt = x.reshape(t, d)
    memt = mem.reshape(batch * n_mem, d)
    pos = positions.reshape(t, 1).astype(F32)
    bf = lambda *ws: [w.astype(BF16) for w in ws]
    row3 = lambda a: a[:, None, :]
    once = dict(pipeline_mode=pl.Buffered(1))
    wg, wu, wd, xa_wq_b, xa_wk_b, xa_wv_b, xa_wo_b = bf(ffn_w_gate, ffn_w_up, ffn_w_down, xa_wq, xa_wk, xa_wv, xa_wo)
    ab_in, ab_uq, ab_ukv, ml_wgt, ml_bgt = _prep_ab_weights(ab_w_in, mla_w_uq, mla_w_ukv, ml_w_gates, ml_b_gates)
    c_in, c_ukt = _prep_c_weights(c_w_in, c_w_uk)
    ab_out, c_out, ml_wq_b, ml_wk_b, c_uq, c_uv = bf(ab_w_out, c_w_out, ml_wq, ml_wk, c_w_uq, c_w_uv)
    mk, mv = _mem_kv(memt, row3(mem_norm_g), xa_wk_b, xa_wv_b, n_mem)

    for l in range(depth):
        if l % 2 == 0:
            e = l // 2
            xt, q, k, v, xm, vm, og = _pre_ab(xt, pos, norm_g, wg, wu, wd, l, e, ab_in, row3(mla_cq_g), row3(mla_ckv_g),
                                              ab_uq, ab_ukv, tm=tm)
            ya = _mla_attn(q, k, v, batch, tq=min(256, seq))
            yb = _mlstm(xm, vm, og, e, ml_conv_w, row3(ml_conv_b), ml_wq_b, ml_wk_b, ml_wgt, ml_bgt,
                        row3(ml_gn_g), row3(ml_skip), batch)
            half = lambda part: pl.BlockSpec((None, A_HEADS * A_V, d), lambda i: (e, part, 0), **once)
            proj = [(ya, ab_out, half(0)), (yb, ab_out, half(1))]
        else:
            o = l // 2
            xt, qlat, ckv, qi, ki, wi = _pre_dsa(xt, norm_g, wg, wu, wd, l, o, c_in, row3(c_cq_g), row3(c_ckv_g),
                                                 c_uq, c_ukt, tm=tm)
            yc = _dsa_attn(qi, wi, qlat, ki, ckv, o, c_uv, batch)
            proj = [(yc, c_out, _stacked(c_out, o, **once))]
        xt = _post_mixer(xt, norm_g, proj, xa_wq_b, xa_wo_b, mk, mv, wg, wu, wd, l, batch, tm=tm)
    return xt.reshape(batch, seq, d)
```

```python
import functools
import math

import jax
import jax.numpy as jnp
from jax import lax
from jax.experimental import pallas as pl
from jax.experimental.pallas import tpu as pltpu

F32 = jnp.float32
BF16 = jnp.bfloat16

EPS = 1e-6
ROPE_BASE = 10000.0
LANES = 128
VMEM_LIMIT = 48 * 1024 * 1024

A_HEADS, A_Q_LORA, A_KV_LORA, A_NOPE, A_ROPE, A_V = 4, 384, 256, 128, 64, 128
B_HEADS, B_HEAD_DIM, B_CONV, B_CHUNK = 4, 128, 4, 128
B_WIDTH = B_HEADS * B_HEAD_DIM
C_HEADS, C_Q_LORA, C_KV_LORA, C_NOPE, C_V = 8, 384, 256, 128, 128
IDX_HEADS, IDX_DIM, TOPK_MAX = 8, 64, 256
X_HEADS, X_HEAD_DIM = 4, 128
Q_BLOCK = 128

LOG2E = math.log2(math.e)
INT_MIN = -(2 ** 31)
KEY_LOWEST_FINITE = INT_MIN + 0x00800000


def _params(*sem):
    return pltpu.CompilerParams(dimension_semantics=sem, vmem_limit_bytes=VMEM_LIMIT)


def _rms(x, g):
    return x * lax.rsqrt(jnp.mean(x * x, axis=-1, keepdims=True) + EPS) * g


def _dot(a, b):
    return jnp.dot(a, b, preferred_element_type=F32)


def _dot_nt(a, b):
    return lax.dot_general(a, b, (((1,), (1,)), ((), ())), preferred_element_type=F32)


def _dot_tn(a, b):
    return lax.dot_general(a, b, (((0,), (0,)), ((), ())), preferred_element_type=F32)


def _full(shape):
    return pl.BlockSpec(shape, lambda *_: (0,) * len(shape))


def _stacked(a, idx, **kw):
    return pl.BlockSpec((None,) + a.shape[1:], lambda *_: (idx,) + (0,) * (a.ndim - 1), **kw)


FUSED_VMEM_LIMIT = 56 * 1024 * 1024
MXU_DIM = 256


def _ffn_chunks(f):
    if f % MXU_DIM:
        return [(0, f)]
    first = (f // MXU_DIM + 1) // 2 * MXU_DIM
    return [(0, first), (first, f)] if first < f else [(0, f)]


def _ffn_val(x, gpre, gpost, wg_ref, wu_ref, wd_ref):
    h = _rms(x, gpre).astype(BF16)
    y = None
    for lo, hi in _ffn_chunks(wg_ref.shape[1]):
        cols = slice(lo, hi)
        g = _dot(h, wg_ref[:, cols])
        u = _dot(h, wu_ref[:, cols])
        a = (g * jax.nn.sigmoid(g) * u).astype(BF16)
        part = _dot(a, wd_ref[cols, :])
        y = part if y is None else y + part
    return x + 0.5 * _rms(y, gpost)


def _rope_tables(pos):
    half = A_ROPE // 2
    lane = lax.broadcasted_iota(jnp.int32, (1, LANES), 1)
    j = (lane % half).astype(F32)
    inv = jnp.exp(j * (-math.log(ROPE_BASE) / half))
    ang = pos * inv
    sin = jnp.where(lane < A_ROPE, jnp.sin(ang), 0.0)
    return jnp.cos(ang), sin


def _rope_apply(r, cos, sin):
    half = A_ROPE // 2
    return r * cos + (pltpu.roll(r, half, axis=1) - pltpu.roll(r, LANES - half, axis=1)) * sin


AB_Z = A_Q_LORA + A_KV_LORA + LANES + 3 * B_WIDTH
A_HEAD_PAD = 2 * LANES


def _ab_prep_val(x, pos, g, w_in_ref, cq_g_ref, ckv_g_ref, w_uq_ref, w_ukv_ref,
                 q_ref, k_ref, v_ref, xm_ref, vm_ref, og_ref):
    h = _rms(x, g).astype(BF16)
    z = _dot(h, w_in_ref[...])
    o = 0
    cq = z[:, o:o + A_Q_LORA]; o += A_Q_LORA
    ckv = z[:, o:o + A_KV_LORA]; o += A_KV_LORA
    kr = z[:, o:o + LANES]; o += LANES
    xm_ref[...] = z[:, o:o + B_WIDTH]; o += B_WIDTH
    vm_ref[...] = z[:, o:o + B_WIDTH]; o += B_WIDTH
    og_ref[...] = z[:, o:o + B_WIDTH]

    cos, sin = _rope_tables(pos)
    scale = (A_NOPE + A_ROPE) ** -0.5 * LOG2E
    q = _dot(_rms(cq, cq_g_ref[...]).astype(BF16), w_uq_ref[...]) * scale
    kv = _dot(_rms(ckv, ckv_g_ref[...]).astype(BF16), w_ukv_ref[...])
    k_rope = _rope_apply(kr, cos, sin).astype(BF16)
    for hd in range(A_HEADS):
        base = hd * A_HEAD_PAD
        q_ref[:, base:base + A_NOPE] = q[:, base:base + A_NOPE].astype(BF16)
        q_ref[:, base + A_NOPE:base + A_HEAD_PAD] = _rope_apply(
            q[:, base + A_NOPE:base + A_HEAD_PAD], cos, sin).astype(BF16)
        k_ref[:, base:base + A_NOPE] = kv[:, hd * A_NOPE:(hd + 1) * A_NOPE].astype(BF16)
        k_ref[:, base + A_NOPE:base + A_HEAD_PAD] = k_rope
    v_ref[...] = kv[:, A_HEADS * A_NOPE:].astype(BF16)


def _pre_ab_body(x_ref, pos_ref, g_ref, wg_ref, wu_ref, wd_ref, w_in_ref, cq_g_ref, ckv_g_ref, w_uq_ref,
                 w_ukv_ref, xo_ref, q_ref, k_ref, v_ref, xm_ref, vm_ref, og_ref):
    x = _ffn_val(x_ref[...], g_ref[0:1, :], g_ref[1:2, :], wg_ref, wu_ref, wd_ref)
    xo_ref[...] = x
    _ab_prep_val(x, pos_ref[...], g_ref[2:3, :], w_in_ref, cq_g_ref, ckv_g_ref, w_uq_ref, w_ukv_ref,
                 q_ref, k_ref, v_ref, xm_ref, vm_ref, og_ref)


def _pre_ab(x, pos, g, wg, wu, wd, layer, e, w_in, cq_g, ckv_g, w_uq, w_ukv, *, tm):
    t, d = x.shape
    f = wg.shape[-1]
    once = dict(pipeline_mode=pl.Buffered(1))
    resident = lambda r, c: pl.BlockSpec((None, None, r, c), lambda i: (layer, 0, 0, 0), **once)
    const = lambda a: _stacked(a, e, **once)
    row = lambda w: pl.BlockSpec((tm, w), lambda i: (i, 0))
    return pl.pallas_call(
        _pre_ab_body,
        grid=(t // tm,),
        in_specs=[row(d), row(1), _stacked(g, layer), resident(d, f), resident(d, f), resident(f, d),
                  const(w_in), const(cq_g), const(ckv_g), const(w_uq), const(w_ukv)],
        out_specs=[row(d), row(A_HEADS * A_HEAD_PAD), row(A_HEADS * A_HEAD_PAD), row(A_HEADS * A_V),
                   row(B_WIDTH), row(B_WIDTH), row(B_WIDTH)],
        out_shape=[jax.ShapeDtypeStruct((t, d), F32),
                   jax.ShapeDtypeStruct((t, A_HEADS * A_HEAD_PAD), BF16),
                   jax.ShapeDtypeStruct((t, A_HEADS * A_HEAD_PAD), BF16),
                   jax.ShapeDtypeStruct((t, A_HEADS * A_V), BF16),
                   jax.ShapeDtypeStruct((t, B_WIDTH), F32),
                   jax.ShapeDtypeStruct((t, B_WIDTH), F32),
                   jax.ShapeDtypeStruct((t, B_WIDTH), F32)],
        compiler_params=pltpu.CompilerParams(dimension_semantics=("parallel",), vmem_limit_bytes=FUSED_VMEM_LIMIT),
        name="pre_ab",
    )(x, pos, g, wg, wu, wd, w_in, cq_g, ckv_g, w_uq, w_ukv)


def _mla_attn_body(tq, q_ref, k_ref, v_ref, o_ref, m_scr, l_scr, acc_scr):
    i = pl.program_id(1)
    heads = range(A_HEADS)
    m_scr[...] = jnp.full(m_scr.shape, -jnp.inf, F32)
    l_scr[...] = jnp.zeros(l_scr.shape, F32)
    acc_scr[...] = jnp.zeros(acc_scr.shape, F32)

    def chunk(start, width, masked):
        widen = lambda a: jnp.concatenate([a] * (width // LANES), axis=1)
        for h in heads:
            s = _dot_nt(q_ref[:, h * A_HEAD_PAD:(h + 1) * A_HEAD_PAD],
                        k_ref[pl.ds(start, width), h * A_HEAD_PAD:(h + 1) * A_HEAD_PAD])
            if masked:
                row = lax.broadcasted_iota(jnp.int32, (tq, width), 0)
                col = lax.broadcasted_iota(jnp.int32, (tq, width), 1)
                s = jnp.where(col <= row, s, -jnp.inf)
            m_old = m_scr[h]
            m_new = jnp.maximum(m_old, jnp.max(s, axis=-1, keepdims=True))
            alpha = jnp.exp2(m_old - m_new)
            p = jnp.exp2(s - widen(m_new))
            l_scr[h] = alpha * l_scr[h] + jnp.sum(p, axis=-1, keepdims=True)
            acc_scr[h] = alpha * acc_scr[h] + _dot(p.astype(BF16), v_ref[pl.ds(start, width), h * A_V:(h + 1) * A_V])
            m_scr[h] = m_new

    def chunk_pair(jp, carry):
        chunk(pl.multiple_of(jp * 2 * tq, 2 * tq), 2 * tq, False)
        return carry

    lax.fori_loop(0, i // 2, chunk_pair, 0)

    @pl.when(i % 2 == 1)
    def _():
        chunk(pl.multiple_of((i - 1) * tq, tq), tq, False)

    chunk(pl.multiple_of(i * tq, tq), tq, True)
    for h in heads:
        o_ref[:, h * A_V:(h + 1) * A_V] = (acc_scr[h] / l_scr[h]).astype(BF16)


def _mla_attn(q, k, v, batch, *, tq):
    t = q.shape[0]
    s = t // batch
    nq = s // tq
    return pl.pallas_call(
        functools.partial(_mla_attn_body, tq),
        grid=(batch, nq),
        in_specs=[pl.BlockSpec((tq, A_HEADS * A_HEAD_PAD), lambda b, i: (b * nq + i, 0)),
                  pl.BlockSpec((s, A_HEADS * A_HEAD_PAD), lambda b, i: (b, 0)),
                  pl.BlockSpec((s, A_HEADS * A_V), lambda b, i: (b, 0))],
        out_specs=pl.BlockSpec((tq, A_HEADS * A_V), lambda b, i: (b * nq + i, 0)),
        out_shape=jax.ShapeDtypeStruct((t, A_HEADS * A_V), BF16),
        scratch_shapes=[pltpu.VMEM((A_HEADS, tq, LANES), F32), pltpu.VMEM((A_HEADS, tq, LANES), F32),
                        pltpu.VMEM((A_HEADS, tq, A_V), F32)],
        compiler_params=_params("parallel", "arbitrary"),
        name="mla_attn",
    )(q, k, v)


def _log_sigmoid(x):
    return jnp.minimum(x, 0.0) - jnp.log(1.0 + jnp.exp(-jnp.abs(x)))


def _mlstm_body(xm_ref, vm_ref, og_ref, conv_w_ref, conv_b_ref, wq_ref, wk_ref, wgt_ref, bgt_ref,
                gn_ref, skip_ref, y_ref, *scratch):
    c = pl.program_id(1)
    L = B_CHUNK
    tail = 8
    per_elem = [scratch[4 * e:4 * e + 4] for e in range(xm_ref.shape[0])]

    @pl.when(c == 0)
    def _():
        for xext_scr, _, state_scr, m_scr in per_elem:
            xext_scr[0:tail, :] = jnp.zeros((tail, B_WIDTH), F32)
            state_scr[...] = jnp.zeros_like(state_scr)
            m_scr[...] = jnp.zeros_like(m_scr)

    lane = lax.broadcasted_iota(jnp.int32, (L, LANES), 1)
    row = lax.broadcasted_iota(jnp.int32, (L, LANES), 0)
    is_f = (lane >= B_HEADS) & (lane < 2 * B_HEADS)
    causal = lax.broadcasted_iota(jnp.int32, (L, L), 1) <= lax.broadcasted_iota(jnp.int32, (L, L), 0)
    for e, (xext_scr, gin_scr, state_scr, m_scr) in enumerate(per_elem):
        _mlstm_chunk(xm_ref.at[e], vm_ref.at[e], og_ref.at[e], conv_w_ref, conv_b_ref, wq_ref, wk_ref, wgt_ref,
                     bgt_ref, gn_ref, skip_ref, y_ref.at[e], xext_scr, gin_scr, state_scr, m_scr,
                     row, is_f, causal)


def _mlstm_chunk(xm_ref, vm_ref, og_ref, conv_w_ref, conv_b_ref, wq_ref, wk_ref, wgt_ref, bgt_ref,
                 gn_ref, skip_ref, y_ref, xext_scr, gin_scr, state_scr, m_scr, row, is_f, causal):
    L = B_CHUNK
    tail = 8
    xm = xm_ref[...]
    xext_scr[tail:tail + L, :] = xm
    conv = conv_b_ref[...] + conv_w_ref[B_CONV - 1:B_CONV, :] * xm
    for tap in range(B_CONV - 1):
        shift = B_CONV - 1 - tap
        conv += conv_w_ref[tap:tap + 1, :] * xext_scr[tail - shift:tail - shift + L, :]
    xext_scr[0:tail, :] = xm[L - tail:, :]
    xc = conv * jax.nn.sigmoid(conv)
    xcb = xc.astype(BF16)

    vm = vm_ref[...]
    qs, ks = [], []
    for h in range(B_HEADS):
        sl = slice(h * B_HEAD_DIM, (h + 1) * B_HEAD_DIM)
        qh = _dot(xcb[:, sl], wq_ref[h])
        kh = _dot(xcb[:, sl], wk_ref[h])
        qs.append(qh)
        ks.append(kh)
        gin_scr[:, h * B_HEAD_DIM:(h + 1) * B_HEAD_DIM] = qh.astype(BF16)
        gin_scr[:, B_WIDTH + h * B_HEAD_DIM:B_WIDTH + (h + 1) * B_HEAD_DIM] = kh.astype(BF16)
    gin_scr[:, 2 * B_WIDTH:] = vm.astype(BF16)
    gates = _dot(gin_scr[...], wgt_ref[...]) + bgt_ref[...]

    glog = jnp.where(is_f, _log_sigmoid(gates), gates)
    bcum = jnp.where(is_f, glog, 0.0)
    step = 1
    while step < L:
        bcum = bcum + jnp.where(row >= step, pltpu.roll(bcum, step, axis=0), 0.0)
        step *= 2
    glog_t = glog.T
    bcum_t = bcum.T

    kscale = B_HEAD_DIM ** -0.5
    ones_blk = jnp.ones((L, LANES), BF16)
    twice = lambda a: jnp.concatenate([a, a], axis=1)
    for h in range(B_HEADS):
        sl = slice(h * B_HEAD_DIM, (h + 1) * B_HEAD_DIM)
        qb = qs[h].astype(BF16)
        k_s = ks[h] * kscale
        v_ext = jnp.concatenate([vm[:, sl].astype(BF16), ones_blk], axis=1)
        li_c = jnp.broadcast_to(glog[:, h:h + 1], (L, LANES))
        b_c = jnp.broadcast_to(bcum[:, B_HEADS + h:B_HEADS + h + 1], (L, LANES))
        li_r = glog_t[h:h + 1, :]
        b_r = bcum_t[B_HEADS + h:B_HEADS + h + 1, :]
        b_tot = jnp.broadcast_to(b_r[:, L - 1:L], (1, LANES))
        m_st = m_scr[h:h + 1, :]
        state = state_scr[h]

        dmat = jnp.where(causal, b_c - b_r + li_r, -jnp.inf)
        m_inter = b_c + m_st
        m_j = jnp.maximum(m_inter, jnp.max(dmat, axis=-1, keepdims=True))
        sc = _dot_nt(qb, k_s.astype(BF16)) * jnp.exp(dmat - m_j)
        inter = jnp.exp(m_inter - m_j)
        ne = twice(inter) * _dot(qb, state.astype(BF16)) + _dot(sc.astype(BF16), v_ext)
        hj = ne[:, :B_HEAD_DIM] / jnp.maximum(jnp.abs(ne[:, B_HEAD_DIM:]), jnp.exp(-m_j))

        g_c = b_tot - b_c + li_c
        g_r = b_tot - b_r + li_r
        m_new = jnp.maximum(b_tot + m_st, jnp.max(g_r, axis=-1, keepdims=True))
        decay = jnp.exp(b_tot + m_st - m_new)
        kw = (k_s * jnp.exp(g_c - m_new)).astype(BF16)
        state_scr[h] = twice(decay) * state + _dot_tn(kw, v_ext)
        m_scr[h:h + 1, :] = m_new

        mu = jnp.mean(hj, axis=-1, keepdims=True)
        var = jnp.mean(jnp.square(hj - mu), axis=-1, keepdims=True)
        hn = (hj - mu) * lax.rsqrt(var + EPS)
        out = (hn * gn_ref[:, sl] + skip_ref[:, sl] * xc[:, sl]) * jax.nn.sigmoid(og_ref[:, sl])
        y_ref[:, sl] = out.astype(BF16)


def _mlstm(xm, vm, og, e, conv_w, conv_b, wq, wk, wgt, bgt, gn_g, skip, batch):
    t = xm.shape[0]
    s = t // batch
    per_step = 2 if batch % 2 == 0 else 1
    seq3 = lambda a: a.reshape(batch, s, B_WIDTH)
    row = pl.BlockSpec((per_step, B_CHUNK, B_WIDTH), lambda b, c: (b, c, 0))
    weights = (conv_w, conv_b, wq, wk, wgt, bgt, gn_g, skip)
    y = pl.pallas_call(
        _mlstm_body,
        grid=(batch // per_step, s // B_CHUNK),
        in_specs=[row, row, row] + [_stacked(a, e) for a in weights],
        out_specs=row,
        out_shape=jax.ShapeDtypeStruct((batch, s, B_WIDTH), BF16),
        scratch_shapes=[pltpu.VMEM((8 + B_CHUNK, B_WIDTH), F32),
                        pltpu.VMEM((B_CHUNK, 3 * B_WIDTH), BF16),
                        pltpu.VMEM((B_HEADS, B_HEAD_DIM, 2 * LANES), F32),
                        pltpu.VMEM((8, LANES), F32)] * per_step,
        compiler_params=_params("parallel", "arbitrary"),
        name="mlstm",
    )(seq3(xm), seq3(vm), seq3(og), *weights)
    return y.reshape(t, B_WIDTH)


C_Z = C_Q_LORA + C_KV_LORA + IDX_HEADS * LANES + LANES + LANES


def _dsa_prep_val(x, g, w_in_ref, cq_g_ref, ckv_g_ref, w_uq_ref, w_ukt_ref,
                  qlat_ref, ckv_ref, qi_ref, ki_ref, wi_ref):
    h = _rms(x, g).astype(BF16)
    z = _dot(h, w_in_ref[...])
    o = 0
    cq = z[:, o:o + C_Q_LORA]; o += C_Q_LORA
    ckv = z[:, o:o + C_KV_LORA]; o += C_KV_LORA
    qi_ref[...] = z[:, o:o + IDX_HEADS * LANES].astype(BF16); o += IDX_HEADS * LANES
    ki_ref[...] = z[:, o:o + LANES].astype(BF16); o += LANES
    wi_ref[...] = z[:, o:o + LANES]
    ckv_ref[...] = _rms(ckv, ckv_g_ref[...]).astype(BF16)
    q = _dot(_rms(cq, cq_g_ref[...]).astype(BF16), w_uq_ref[...]).astype(BF16)
    scale = C_NOPE ** -0.5 * LOG2E
    for hd in range(C_HEADS):
        ql = _dot(q[:, hd * C_NOPE:(hd + 1) * C_NOPE], w_ukt_ref[hd]) * scale
        qlat_ref[:, hd * C_KV_LORA:(hd + 1) * C_KV_LORA] = ql.astype(BF16)


def _pre_dsa_body(x_ref, g_ref, wg_ref, wu_ref, wd_ref, w_in_ref, cq_g_ref, ckv_g_ref, w_uq_ref, w_ukt_ref,
                  xo_ref, qlat_ref, ckv_ref, qi_ref, ki_ref, wi_ref):
    x = _ffn_val(x_ref[...], g_ref[0:1, :], g_ref[1:2, :], wg_ref, wu_ref, wd_ref)
    xo_ref[...] = x
    _dsa_prep_val(x, g_ref[2:3, :], w_in_ref, cq_g_ref, ckv_g_ref, w_uq_ref, w_ukt_ref,
                  qlat_ref, ckv_ref, qi_ref, ki_ref, wi_ref)


def _pre_dsa(x, g, wg, wu, wd, layer, o, w_in, cq_g, ckv_g, w_uq, w_ukt, *, tm):
    t, d = x.shape
    f = wg.shape[-1]
    once = dict(pipeline_mode=pl.Buffered(1))
    resident = lambda r, c: pl.BlockSpec((None, None, r, c), lambda i: (layer, 0, 0, 0), **once)
    const = lambda a: _stacked(a, o, **once)
    row = lambda w: pl.BlockSpec((tm, w), lambda i: (i, 0))
    return pl.pallas_call(
        _pre_dsa_body,
        grid=(t // tm,),
        in_specs=[row(d), _stacked(g, layer), resident(d, f), resident(d, f), resident(f, d),
                  const(w_in), const(cq_g), const(ckv_g), const(w_uq), const(w_ukt)],
        out_specs=[row(d), row(C_HEADS * C_KV_LORA), row(C_KV_LORA), row(IDX_HEADS * LANES), row(LANES), row(LANES)],
        out_shape=[jax.ShapeDtypeStruct((t, d), F32),
                   jax.ShapeDtypeStruct((t, C_HEADS * C_KV_LORA), BF16),
                   jax.ShapeDtypeStruct((t, C_KV_LORA), BF16),
                   jax.ShapeDtypeStruct((t, IDX_HEADS * LANES), BF16),
                   jax.ShapeDtypeStruct((t, LANES), BF16),
                   jax.ShapeDtypeStruct((t, LANES), F32)],
        compiler_params=pltpu.CompilerParams(dimension_semantics=("parallel",), vmem_limit_bytes=FUSED_VMEM_LIMIT),
        name="pre_dsa",
    )(x, g, wg, wu, wd, w_in, cq_g, ckv_g, w_uq, w_ukt)


INT_MAX = 2 ** 31 - 1
DSA_VMEM_LIMIT = 56 * 1024 * 1024


def _key_to_f32(key):
    return pltpu.bitcast(key ^ ((key >> 31) & 0x7FFFFFFF), F32)


def _dsa_attn_body(topk, seq, qi_ref, wi_ref, qlat_ref, ki_ref, ckv_ref, w_uv_ref, o_ref,
                   score_scr, key_scr, cnt_scr, bound_scr, m_scr, l_scr, acc_scr):
    Q = Q_BLOCK
    H = C_HEADS
    nb = seq // Q
    first = topk // Q
    col_iota = lax.broadcasted_iota(jnp.int32, (Q, Q), 1)
    ones = jnp.ones((Q, LANES), BF16)
    topk_f = float(topk)

    qa = 2 * Q
    qa_col = lax.broadcasted_iota(jnp.int32, (qa, qa), 1)
    qa_row = lax.broadcasted_iota(jnp.int32, (qa, qa), 0)

    def score_block(blk, _):
        r0 = pl.multiple_of(blk * qa, qa)
        wi = wi_ref[pl.ds(r0, qa), :]
        wb = [jnp.broadcast_to(wi[:, h:h + 1], (qa, qa)) for h in range(IDX_HEADS)]

        def tile(c, _):
            k0 = pl.multiple_of(c * qa, qa)
            keys = ki_ref[pl.ds(k0, qa), :]
            score = None
            for h in range(IDX_HEADS):
                rel = jnp.maximum(_dot_nt(qi_ref[pl.ds(r0, qa), h * LANES:(h + 1) * LANES], keys), 0.0)
                score = rel * wb[h] if score is None else score + rel * wb[h]
            score = jnp.where(k0 + qa_col <= r0 + qa_row, score, -jnp.inf)
            score_scr[pl.ds(r0, qa), pl.ds(k0, qa)] = score
            return 0

        lax.fori_loop(0, blk + 1, tile, 0)
        return 0

    lax.fori_loop(0, seq // qa, score_block, 0)

    if first > 0:
        key_scr[0:first * Q, :] = jnp.full((first * Q, LANES), KEY_LOWEST_FINITE, jnp.int32)
        cnt_scr[0:first * Q, :] = jnp.full((first * Q, LANES), topk_f, F32)
    for i in range(first, nb):
        key_scr[i * Q:(i + 1) * Q, :] = jnp.full((Q, LANES), INT_MIN, jnp.int32)
        cnt_scr[i * Q:(i + 1) * Q, :] = jnp.full((Q, LANES), float((i + 1) * Q), F32)

    def bit_step(b, _):
        bit = jnp.left_shift(jnp.int32(1), 31 - b)
        for i in range(first, nb):
            rows = slice(i * Q, (i + 1) * Q)
            key = key_scr[rows, :]
            cand = jnp.where(b == 0, jnp.zeros_like(key), key | bit)
            cand_f = _key_to_f32(cand)
            hits = jnp.zeros((Q, LANES), F32)
            for c in range(i + 1):
                hits += jnp.where(score_scr[rows, c * Q:(c + 1) * Q] >= cand_f, 1.0, 0.0)
            cnt = _dot(hits.astype(BF16), ones)
            ok = cnt >= topk_f
            key_scr[rows, :] = jnp.where(ok, cand, key)
            cnt_scr[rows, :] = jnp.where(ok, cnt, cnt_scr[rows, :])
        return 0

    lax.fori_loop(0, 32, bit_step, 0)

    bound_scr[...] = jnp.full(bound_scr.shape, INT_MAX, jnp.int32)

    idx_bits = (seq - 1).bit_length()

    def tie_block(i, _):
        r0 = pl.multiple_of(i * Q, Q)

        @pl.when(jnp.max(cnt_scr[pl.ds(r0, Q), :]) > topk_f)
        def _():
            thr = _key_to_f32(key_scr[pl.ds(r0, Q), :])

            def count(pred_fn):
                def body(c, hits):
                    k0 = pl.multiple_of(c * Q, Q)
                    return hits + jnp.where(pred_fn(score_scr[pl.ds(r0, Q), pl.ds(k0, Q)], k0), 1.0, 0.0)
                return _dot(lax.fori_loop(0, i + 1, body, jnp.zeros((Q, LANES), F32)).astype(BF16), ones)

            need = topk_f - count(lambda sc, k0: sc > thr)

            def pos_step(b, bound):
                cand = bound - jnp.right_shift(jnp.int32(1 << (idx_bits - 1)), b)
                cnt = count(lambda sc, k0: (sc == thr) & (k0 + col_iota <= cand))
                return jnp.where(cnt >= need, cand, bound)

            all_keys = jnp.full((Q, LANES), (1 << idx_bits) - 1, jnp.int32)
            bound_scr[pl.ds(r0, Q), :] = lax.fori_loop(0, idx_bits, pos_step, all_keys)
        return 0

    lax.fori_loop(first, nb, tie_block, 0)

    twice = lambda a: jnp.concatenate([a, a], axis=1)

    def attn_block(blk, _):
        r0 = pl.multiple_of(blk * qa, qa)
        qlat = jnp.concatenate([qlat_ref[pl.ds(r0, qa), h * C_KV_LORA:(h + 1) * C_KV_LORA] for h in range(H)], axis=0)
        thr = _key_to_f32(key_scr[pl.ds(r0, qa), :])
        bound = bound_scr[pl.ds(r0, qa), :]
        m_scr[...] = jnp.full(m_scr.shape, jnp.finfo(F32).min, F32)
        l_scr[...] = jnp.zeros(l_scr.shape, F32)
        acc_scr[...] = jnp.zeros(acc_scr.shape, F32)

        def tile(k0, width):
            wide = lambda a: jnp.concatenate([a] * (width // LANES), axis=1)
            sc = score_scr[pl.ds(r0, qa), pl.ds(k0, width)]
            kidx = k0 + lax.broadcasted_iota(jnp.int32, (qa, width), 1)
            thr_w = wide(thr)
            sel = (sc > thr_w) | ((sc == thr_w) & (kidx <= wide(bound)))
            ckv = ckv_ref[pl.ds(k0, width), :]
            logits = _dot_nt(qlat, ckv)
            for h in range(H):
                rs = slice(h * qa, (h + 1) * qa)
                s = jnp.where(sel, logits[rs], -jnp.inf)
                m_old = m_scr[rs]
                m_new = jnp.maximum(m_old, jnp.max(s, axis=-1, keepdims=True))
                alpha = jnp.exp2(m_old - m_new)
                p = jnp.exp2(s - wide(m_new))
                l_scr[rs] = alpha * l_scr[rs] + jnp.sum(p, axis=-1, keepdims=True)
                acc_scr[rs] = twice(alpha) * acc_scr[rs] + _dot(p.astype(BF16), ckv)
                m_scr[rs] = m_new

        def tile_pair(cp, carry):
            tile(pl.multiple_of(cp * 2 * qa, 2 * qa), 2 * qa)
            return carry

        lax.fori_loop(0, (blk + 1) // 2, tile_pair, 0)

        @pl.when(blk % 2 == 0)
        def _():
            tile(pl.multiple_of(blk * qa, qa), qa)

        for h in range(H):
            rs = slice(h * qa, (h + 1) * qa)
            o_lat = (acc_scr[rs] / twice(l_scr[rs])).astype(BF16)
            o_ref[pl.ds(r0, qa), h * C_V:(h + 1) * C_V] = _dot(o_lat, w_uv_ref[h]).astype(BF16)
        return 0

    lax.fori_loop(0, seq // qa, attn_block, 0)


def _dsa_attn(qi, wi, qlat, ki, ckv, o, w_uv, batch):
    t = qi.shape[0]
    s = t // batch
    topk = min(TOPK_MAX, s // 4)
    assert topk % Q_BLOCK == 0 and s % (2 * Q_BLOCK) == 0 and s // Q_BLOCK <= 256
    seq = lambda w: pl.BlockSpec((s, w), lambda b: (b, 0), pipeline_mode=pl.Buffered(1))
    qa = 2 * Q_BLOCK
    return pl.pallas_call(
        functools.partial(_dsa_attn_body, topk, s),
        grid=(batch,),
        in_specs=[seq(IDX_HEADS * LANES), seq(LANES), seq(C_HEADS * C_KV_LORA), seq(LANES), seq(C_KV_LORA),
                  _stacked(w_uv, o)],
        out_specs=pl.BlockSpec((s, C_HEADS * C_V), lambda b: (b, 0)),
        out_shape=jax.ShapeDtypeStruct((t, C_HEADS * C_V), BF16),
        scratch_shapes=[pltpu.VMEM((s, s), F32),
                        pltpu.VMEM((s, LANES), jnp.int32),
                        pltpu.VMEM((s, LANES), F32),
                        pltpu.VMEM((s, LANES), jnp.int32),
                        pltpu.VMEM((C_HEADS * qa, LANES), F32),
                        pltpu.VMEM((C_HEADS * qa, LANES), F32),
                        pltpu.VMEM((C_HEADS * qa, C_KV_LORA), F32)],
        compiler_params=pltpu.CompilerParams(dimension_semantics=("arbitrary",), vmem_limit_bytes=DSA_VMEM_LIMIT),
        name="dsa_attn",
    )(qi, wi, qlat, ki, ckv, w_uv)


def _mem_kv_body(mem_ref, g_ref, wk_ref, wv_ref, k_ref, v_ref):
    mn = _rms(mem_ref[...], g_ref[...]).astype(BF16)
    k_ref[...] = _dot(mn, wk_ref[...]).astype(BF16)
    v_ref[...] = _dot(mn, wv_ref[...]).astype(BF16)


def _mem_kv(mem, g, wk, wv, n_mem):
    rows, d = mem.shape
    depth, _, w = wk.shape
    wspec = pl.BlockSpec((None, d, w), lambda l, b: (l, 0, 0))
    out = pl.BlockSpec((None, n_mem, w), lambda l, b: (l, b, 0))
    return pl.pallas_call(
        _mem_kv_body,
        grid=(depth, rows // n_mem),
        in_specs=[pl.BlockSpec((n_mem, d), lambda l, b: (b, 0)), pl.BlockSpec((None, 1, d), lambda l, b: (l, 0, 0)),
                  wspec, wspec],
        out_specs=[out, out],
        out_shape=[jax.ShapeDtypeStruct((depth, rows, w), BF16)] * 2,
        compiler_params=_params("parallel", "parallel"),
        name="mem_kv",
    )(mem, g, wk, wv)


def _xattn_val(x, gpre, gpost, wq_ref, wo_ref, k_ref, v_ref, att_scr):
    h = _rms(x, gpre).astype(BF16)
    q = (_dot(h, wq_ref[...]) * (X_HEAD_DIM ** -0.5 * LOG2E)).astype(BF16)
    for hd in range(X_HEADS):
        sl = slice(hd * X_HEAD_DIM, (hd + 1) * X_HEAD_DIM)
        s = _dot_nt(q[:, sl], k_ref[:, sl])
        p = jnp.exp2(s - jnp.max(s, axis=-1, keepdims=True))
        o = _dot(p.astype(BF16), v_ref[:, sl]) / jnp.sum(p, axis=-1, keepdims=True)
        att_scr[:, sl] = o.astype(BF16)
    return x + _rms(_dot(att_scr[...], wo_ref[...]), gpost)


def _post_body(n_in, x_ref, g_ref, *refs):
    a_refs, w_refs = refs[:n_in], refs[n_in:2 * n_in]
    wq_ref, wo_ref, k_ref, v_ref, wg_ref, wu_ref, wd_ref, o_ref, att_scr = refs[2 * n_in:]
    y = _dot(a_refs[0][...], w_refs[0][...])
    for a_ref, w_ref in zip(a_refs[1:], w_refs[1:]):
        y += _dot(a_ref[...], w_ref[...])
    x = x_ref[...] + _rms(y, g_ref[3:4, :])
    x = _xattn_val(x, g_ref[4:5, :], g_ref[5:6, :], wq_ref, wo_ref, k_ref, v_ref, att_scr)
    o_ref[...] = _ffn_val(x, g_ref[6:7, :], g_ref[7:8, :], wg_ref, wu_ref, wd_ref)


def _post_mixer(x, g, proj, xa_wq, xa_wo, mk, mv, wg, wu, wd, layer, batch, *, tm):
    t, d = x.shape
    acts = [a for a, _, _ in proj]
    w = xa_wq.shape[-1]
    f = wg.shape[-1]
    n_mem = mk.shape[1] // batch
    tiles_per_batch = t // batch // tm
    once = dict(pipeline_mode=pl.Buffered(1))
    resident = lambda r, c: pl.BlockSpec((None, None, r, c), lambda i: (layer, 1, 0, 0), **once)
    mem_spec = pl.BlockSpec((None, n_mem, w), lambda i: (layer, i // tiles_per_batch, 0))
    return pl.pallas_call(
        functools.partial(_post_body, len(proj)),
        grid=(t // tm,),
        in_specs=[pl.BlockSpec((tm, d), lambda i: (i, 0)), _stacked(g, layer)]
        + [pl.BlockSpec((tm, a.shape[1]), lambda i: (i, 0)) for a in acts]
        + [spec for _, _, spec in proj]
        + [_stacked(xa_wq, layer, **once), _stacked(xa_wo, layer, **once), mem_spec, mem_spec,
           resident(d, f), resident(d, f), resident(f, d)],
        out_specs=pl.BlockSpec((tm, d), lambda i: (i, 0)),
        out_shape=jax.ShapeDtypeStruct((t, d), F32),
        scratch_shapes=[pltpu.VMEM((tm, w), BF16)],
        compiler_params=pltpu.CompilerParams(dimension_semantics=("parallel",), vmem_limit_bytes=FUSED_VMEM_LIMIT),
        name="post_mixer",
    )(x, g, *acts, *[wt for _, wt, _ in proj], xa_wq, xa_wo, mk, mv, wg, wu, wd)


def _pad_last(a, width):
    return jnp.pad(a, [(0, 0)] * (a.ndim - 1) + [(0, width - a.shape[-1])])


def _prep_ab_weights(w_in, w_uq, w_ukv, w_gates, b_gates):
    n = w_in.shape[0]
    o = A_Q_LORA + A_KV_LORA
    w_in_p = jnp.concatenate([w_in[..., :o], _pad_last(w_in[..., o:o + A_ROPE], LANES), w_in[..., o + A_ROPE:]], axis=-1)
    uq = _pad_last(w_uq.reshape(n, A_Q_LORA, A_HEADS, A_NOPE + A_ROPE), A_HEAD_PAD)
    uq = uq.reshape(n, A_Q_LORA, A_HEADS * A_HEAD_PAD)
    ukv = w_ukv.reshape(n, A_KV_LORA, A_HEADS, A_NOPE + A_V)
    ukv = jnp.concatenate([ukv[..., :A_NOPE].reshape(n, A_KV_LORA, -1), ukv[..., A_NOPE:].reshape(n, A_KV_LORA, -1)],
                          axis=-1)
    return (w_in_p.astype(BF16), uq.astype(BF16), ukv.astype(BF16),
            _pad_last(w_gates, LANES).astype(BF16), _pad_last(b_gates[:, None, :], LANES))


def _prep_c_weights(w_in, w_uk):
    n = w_in.shape[0]
    o = C_Q_LORA + C_KV_LORA
    qi = _pad_last(w_in[..., o:o + IDX_HEADS * IDX_DIM].reshape(n, -1, IDX_HEADS, IDX_DIM), LANES)
    qi = qi.reshape(n, -1, IDX_HEADS * LANES)
    o2 = o + IDX_HEADS * IDX_DIM
    w_in_p = jnp.concatenate([w_in[..., :o], qi, _pad_last(w_in[..., o2:o2 + IDX_DIM], LANES),
                              _pad_last(w_in[..., o2 + IDX_DIM:], LANES)], axis=-1)
    return w_in_p.astype(BF16), jnp.swapaxes(w_uk, 2, 3).astype(BF16)


def kernel(x, mem, positions, norm_g, mem_norm_g, ffn_w_gate, ffn_w_up, ffn_w_down, xa_wq, xa_wk, xa_wv, xa_wo, ab_w_in, ab_w_out, mla_cq_g, mla_ckv_g, mla_w_uq, mla_w_ukv, ml_conv_w, ml_conv_b, ml_wq, ml_wk, ml_w_gates, ml_b_gates, ml_gn_g, ml_skip, c_w_in, c_w_out, c_cq_g, c_ckv_g, c_w_uq, c_w_uk, c_w_uv):
    batch, seq, d = x.shape
    depth = norm_g.shape[0]
    n_mem = mem.shape[1]
    t = batch * seq
    tm = min(512, seq)

    xt = x.reshape(t, d)
    memt = mem.reshape(batch * n_mem, d)
    pos = positions.reshape(t, 1).astype(F32)
    bf = lambda *ws: [w.astype(BF16) for w in ws]
    row3 = lambda a: a[:, None, :]
    once = dict(pipeline_mode=pl.Buffered(1))
    wg, wu, wd, xa_wq_b, xa_wk_b, xa_wv_b, xa_wo_b = bf(ffn_w_gate, ffn_w_up, ffn_w_down, xa_wq, xa_wk, xa_wv, xa_wo)
    ab_in, ab_uq, ab_ukv, ml_wgt, ml_bgt = _prep_ab_weights(ab_w_in, mla_w_uq, mla_w_ukv, ml_w_gates, ml_b_gates)
    c_in, c_ukt = _prep_c_weights(c_w_in, c_w_uk)
    ab_out, c_out, ml_wq_b, ml_wk_b, c_uq, c_uv = bf(ab_w_out, c_w_out, ml_wq, ml_wk, c_w_uq, c_w_uv)
    mk, mv = _mem_kv(memt, row3(mem_norm_g), xa_wk_b, xa_wv_b, n_mem)

    for l in range(depth):
        if l % 2 == 0:
            e = l // 2
            xt, q, k, v, xm, vm, og = _pre_ab(xt, pos, norm_g, wg, wu, wd, l, e, ab_in, row3(mla_cq_g), row3(mla_ckv_g),
                                              ab_uq, ab_ukv, tm=tm)
            ya = _mla_attn(q, k, v, batch, tq=min(512, seq))
            yb = _mlstm(xm, vm, og, e, ml_conv_w, row3(ml_conv_b), ml_wq_b, ml_wk_b, ml_wgt, ml_bgt,
                        row3(ml_gn_g), row3(ml_skip), batch)
            half = lambda part: pl.BlockSpec((None, A_HEADS * A_V, d), lambda i: (e, part, 0), **once)
            proj = [(ya, ab_out, half(0)), (yb, ab_out, half(1))]
        else:
            o = l // 2
            xt, qlat, ckv, qi, ki, wi = _pre_dsa(xt, norm_g, wg, wu, wd, l, o, c_in, row3(c_cq_g), row3(c_ckv_g),
                                                 c_uq, c_ukt, tm=tm)
            yc = _dsa_attn(qi, wi, qlat, ki, ckv, o, c_uv, batch)
            proj = [(yc, c_out, _stacked(c_out, o, **once))]
        xt = _post_mixer(xt, norm_g, proj, xa_wq_b, xa_wo_b, mk, mv, wg, wu, wd, l, batch, tm=tm)
    return xt.reshape(batch, seq, d)
```
